```python
import math
import jax
import jax.numpy as jnp
from jax import lax
import numpy as np

D_MODEL = 1024
BATCH = 2
SEQ = 16384
DEPTH = 2

N_EVEN = (DEPTH + 1) // 2
N_ODD = DEPTH // 2

SSD_HEADS = 16
SSD_HEAD_DIM = 64
SSD_INNER = SSD_HEADS * SSD_HEAD_DIM
SSD_GROUPS = 4
SSD_STATE = 128
SSD_CONV = 4
SSD_CONV_PAD = (2, 1)
SSD_CHUNK = 128
SSD_XBC = SSD_INNER + 2 * SSD_GROUPS * SSD_STATE
SC_WIDTH = 1024
SC_CONV = 3
MLA_HEADS = 8
MLA_Q_RANK = 256
MLA_KV_RANK = 128
MLA_NOPE = 64
MLA_ROPE = 32
MLA_V = 64
MLA_WIDTH = MLA_HEADS * MLA_V
ROPE_THETA = 10000.0
ATTN_BLOCK = 128
ATTN_SCALE = (MLA_NOPE + MLA_ROPE) ** -0.5
POOL_WINDOWS = (2, 4, 8, 16)
POOL_GROUP = 128
POOL_WIDTH = POOL_GROUP * len(POOL_WINDOWS)
EPS = 1e-5
ALPHA = (2 * DEPTH) ** 0.25
BETA = (8 * DEPTH) ** -0.25

EVEN_SIZES = (SSD_INNER, SSD_XBC, 2 * SSD_HEADS, SC_WIDTH, SC_WIDTH, SC_WIDTH, SC_WIDTH)
ODD_SIZES = (MLA_Q_RANK, MLA_KV_RANK, MLA_ROPE, MLA_WIDTH, POOL_WIDTH, POOL_WIDTH)
EVEN_PROJ = sum(EVEN_SIZES)
ODD_PROJ = sum(ODD_SIZES)
EVEN_OUT = SSD_INNER + SC_WIDTH
ODD_OUT = MLA_WIDTH + POOL_WIDTH

kernel_name = 'hybrid_ssd_shortconv_mla_pool_encoder'

F32 = jnp.float32


def _split(t, sizes):
    return jnp.split(t, [int(s) for s in np.cumsum(sizes)[:-1]], axis=-1)


def _layer_norm(x, g, b):
    xf = x.astype(F32)
    mu = jnp.mean(xf, -1, keepdims=True)
    var = jnp.mean(jnp.square(xf - mu), -1, keepdims=True)
    return ((xf - mu) * lax.rsqrt(var + EPS) * g + b).astype(x.dtype)


def _rms_norm(x, g):
    xf = x.astype(F32)
    return (xf * lax.rsqrt(jnp.mean(xf * xf, -1, keepdims=True) + EPS) * g).astype(x.dtype)


def _depthwise_conv(u, w, pad):
    return lax.conv_general_dilated(
        u, w[:, None, :].astype(u.dtype), window_strides=(1,), padding=(pad,),
        dimension_numbers=('NWC', 'WIO', 'NWC'), feature_group_count=u.shape[-1])


def _segsum_exp(cs):
    q = cs.shape[-1]
    mask = jnp.tril(jnp.ones((q, q), bool))
    return jnp.exp(jnp.where(mask, cs[..., :, None] - cs[..., None, :], -jnp.inf))


def _ssd_chunked(x, dt, a, bm, cm):
    b, S, H, P = x.shape
    G, N = bm.shape[2], bm.shape[3]
    R = H // G
    nc, Q = S // SSD_CHUNK, SSD_CHUNK
    xd = (x.astype(F32) * dt[..., None]).reshape(b, nc, Q, G, R, P)
    cs = jnp.cumsum(jnp.transpose((dt * a).reshape(b, nc, Q, G, R), (0, 1, 3, 4, 2)), axis=-1)
    bq = bm.astype(F32).reshape(b, nc, Q, G, N)
    cq = cm.astype(F32).reshape(b, nc, Q, G, N)
    cb = jnp.einsum('bclgn,bcsgn->bcgls', cq, bq)
    y_diag = jnp.einsum('bcgls,bcgrls,bcsgrp->bclgrp', cb, _segsum_exp(cs), xd)
    decay_states = jnp.exp(cs[..., -1:] - cs)
    states = jnp.einsum('bclgn,bcgrl,bclgrp->bcgrpn', bq, decay_states, xd)
    chunk_tot = cs[..., -1]

    def step(h, inp):
        st, tot = inp
        return h * jnp.exp(tot)[..., None, None] + st, h

    h0 = jnp.zeros((b, G, R, P, N), F32)
    _, h_in = lax.scan(step, h0, (jnp.moveaxis(states, 1, 0), jnp.moveaxis(chunk_tot, 1, 0)))
    h_in = jnp.moveaxis(h_in, 0, 1)
    y_off = jnp.einsum('bclgn,bcgrpn,bcgrl->bclgrp', cq, h_in, jnp.exp(cs))
    return (y_diag + y_off).reshape(b, S, H, P)


def _ssd_bidirectional(xs, bm, cm, dt_raw, a_log, dt_bias, d_skip):
    def one_dir(x_, b_, c_, dtr, k):
        dt = jax.nn.softplus(dtr.astype(F32) + dt_bias[k].astype(F32))
        a = -jnp.exp(a_log[k].astype(F32))
        return _ssd_chunked(x_, dt, a, b_, c_) + d_skip[k].astype(F32)[:, None] * x_.astype(F32)

    flip = lambda t: jnp.flip(t, axis=1)
    y_f = one_dir(xs, bm, cm, dt_raw[:, :, 0], 0)
    y_b = flip(one_dir(flip(xs), flip(bm), flip(cm), flip(dt_raw[:, :, 1]), 1))
    return y_f + y_b


def _even_layer(x, w_in, conv_w, conv_b, a_log, dt_bias, d_skip, norm_g, sc_conv_w, w_out):
    b, S, _ = x.shape
    proj = x @ w_in
    z, xbc, dt_raw, sc_bg, sc_cg, sc_h, sc_gate = _split(proj, EVEN_SIZES)
    xbc = jax.nn.silu(_depthwise_conv(xbc, conv_w, SSD_CONV_PAD) + conv_b)
    xs, bm, cm = _split(xbc, (SSD_INNER, SSD_GROUPS * SSD_STATE, SSD_GROUPS * SSD_STATE))
    y_a = _ssd_bidirectional(
        xs.reshape(b, S, SSD_HEADS, SSD_HEAD_DIM),
        bm.reshape(b, S, SSD_GROUPS, SSD_STATE), cm.reshape(b, S, SSD_GROUPS, SSD_STATE),
        dt_raw.reshape(b, S, 2, SSD_HEADS), a_log, dt_bias, d_skip)
    y_a = _rms_norm(y_a.reshape(b, S, SSD_INNER).astype(x.dtype) * jax.nn.silu(z), norm_g)
    y_b = sc_bg * _depthwise_conv(sc_cg * sc_h, sc_conv_w, (1, 1)) * jax.nn.silu(sc_gate)
    return jnp.concatenate([y_a, y_b], axis=-1) @ w_out


def _rope(t, cos, sin):
    half = t.shape[-1] // 2
    t1, t2 = t[..., :half].astype(F32), t[..., half:].astype(F32)
    return jnp.concatenate([t1 * cos - t2 * sin, t1 * sin + t2 * cos], axis=-1).astype(t.dtype)


def _mla_attention(q_nope, q_rope, k_nope, k_rope, v):
    b, S, H, _ = q_nope.shape
    nb = S // ATTN_BLOCK
    to_blocks = lambda t: jnp.moveaxis(t.reshape((b, nb, ATTN_BLOCK) + t.shape[2:]), 1, 0)

    def block(qs):
        qn, qr = qs
        s = (jnp.einsum('bqhd,bkhd->bhqk', qn, k_nope, preferred_element_type=F32)
             + jnp.einsum('bqhr,bkr->bhqk', qr, k_rope, preferred_element_type=F32)) * ATTN_SCALE
        p = jax.nn.softmax(s, axis=-1).astype(v.dtype)
        return jnp.einsum('bhqk,bkhd->bqhd', p, v)

    o = lax.map(block, (to_blocks(q_nope), to_blocks(q_rope)))
    return jnp.moveaxis(o, 0, 1).reshape(b, S, H * MLA_V)


def _multiscale_pool(u):
    S = u.shape[1]
    cs = jnp.pad(jnp.cumsum(u.astype(F32), axis=1), ((0, 0), (1, 0), (0, 0)))
    pos = jnp.arange(S)
    outs = []
    for gi, w in enumerate(POOL_WINDOWS):
        lo = jnp.clip(pos - w // 2, 0, S)
        hi = jnp.clip(pos + w - w // 2, 0, S)
        cg = cs[..., gi * POOL_GROUP:(gi + 1) * POOL_GROUP]
        cnt = (hi - lo).astype(F32)[None, :, None]
        outs.append((jnp.take(cg, hi, axis=1) - jnp.take(cg, lo, axis=1)) / cnt)
    return (jnp.concatenate(outs, axis=-1) - u.astype(F32)).astype(u.dtype)


def _odd_layer(x, positions, w_in, q_norm_g, w_uq, kv_norm_g, w_ukv, pool_w, pool_scale, w_out):
    b, S, _ = x.shape
    proj = x @ w_in
    cq, ckv, k_rope, gate_c, u_d, gate_d = _split(proj, ODD_SIZES)
    q = (_rms_norm(cq, q_norm_g) @ w_uq).reshape(b, S, MLA_HEADS, MLA_NOPE + MLA_ROPE)
    kv = (_rms_norm(ckv, kv_norm_g) @ w_ukv).reshape(b, S, MLA_HEADS, MLA_NOPE + MLA_V)
    q_nope, q_rope = q[..., :MLA_NOPE], q[..., MLA_NOPE:]
    k_nope, v = kv[..., :MLA_NOPE], kv[..., MLA_NOPE:]
    half = MLA_ROPE // 2
    inv_freq = ROPE_THETA ** (-jnp.arange(half, dtype=F32) / half)
    ang = positions.astype(F32)[..., None] * inv_freq
    cos, sin = jnp.cos(ang), jnp.sin(ang)
    q_rope = _rope(q_rope, cos[:, :, None, :], sin[:, :, None, :])
    k_rope = _rope(k_rope, cos, sin)
    y_c = _mla_attention(q_nope, q_rope, k_nope, k_rope, v) * jax.nn.silu(gate_c)
    pooled = _multiscale_pool(u_d).reshape(b, S, len(POOL_WINDOWS), POOL_GROUP)
    y_d = jnp.einsum('bsgc,gcd->bsgd', pooled, pool_w).reshape(b, S, POOL_WIDTH)
    y_d = y_d * pool_scale * jax.nn.silu(gate_d)
    return jnp.concatenate([y_c, y_d], axis=-1) @ w_out


def setup_inputs(seed: int = 0) -> dict:
    key = jax.random.key(seed)
    ks = iter(jax.random.split(key, 32))
    nrm = lambda shape, scale: jax.random.normal(next(ks), shape, F32) * scale
    NE, NO = N_EVEN, N_ODD
    x = jax.random.normal(next(ks), (BATCH, SEQ, D_MODEL), F32)
    positions = (jnp.arange(SEQ, dtype=jnp.int32)[None, :]
                 + jax.random.randint(next(ks), (BATCH, 1), 0, 4096, dtype=jnp.int32))
    dt0 = jnp.exp(jax.random.uniform(next(ks), (NE, 2, SSD_HEADS), F32, math.log(1e-3), math.log(1e-1)))
    ev_dt_bias = dt0 + jnp.log(-jnp.expm1(-dt0))
    ev_a_log = jnp.log(jax.random.uniform(next(ks), (NE, 2, SSD_HEADS), F32, 1.0, 16.0))
    return {
        'x': x,
        'positions': positions,
        'ev_w_in': nrm((NE, D_MODEL, EVEN_PROJ), D_MODEL ** -0.5),
        'ev_conv_w': nrm((NE, SSD_CONV, SSD_XBC), SSD_CONV ** -0.5),
        'ev_conv_b': nrm((NE, SSD_XBC), 0.02),
        'ev_a_log': ev_a_log,
        'ev_dt_bias': ev_dt_bias,
        'ev_d_skip': 1.0 + nrm((NE, 2, SSD_HEADS), 0.05),
        'ev_norm_g': 1.0 + nrm((NE, SSD_INNER), 0.02),
        'ev_sc_conv_w': nrm((NE, SC_CONV, SC_WIDTH), SC_CONV ** -0.5),
        'ev_w_out': nrm((NE, EVEN_OUT, D_MODEL), BETA * EVEN_OUT ** -0.5),
        'ev_ln_g': 1.0 + nrm((NE, D_MODEL), 0.02),
        'ev_ln_b': nrm((NE, D_MODEL), 0.02),
        'od_w_in': nrm((NO, D_MODEL, ODD_PROJ), D_MODEL ** -0.5),
        'od_q_norm_g': 1.0 + nrm((NO, MLA_Q_RANK), 0.02),
        'od_w_uq': nrm((NO, MLA_Q_RANK, MLA_HEADS * (MLA_NOPE + MLA_ROPE)), MLA_Q_RANK ** -0.5),
        'od_kv_norm_g': 1.0 + nrm((NO, MLA_KV_RANK), 0.02),
        'od_w_ukv': nrm((NO, MLA_KV_RANK, MLA_HEADS * (MLA_NOPE + MLA_V)), MLA_KV_RANK ** -0.5),
        'od_pool_w': nrm((NO, len(POOL_WINDOWS), POOL_GROUP, POOL_GROUP), POOL_GROUP ** -0.5),
        'od_pool_scale': 1.0 + nrm((NO, POOL_WIDTH), 0.02),
        'od_w_out': nrm((NO, ODD_OUT, D_MODEL), BETA * ODD_OUT ** -0.5),
        'od_ln_g': 1.0 + nrm((NO, D_MODEL), 0.02),
        'od_ln_b': nrm((NO, D_MODEL), 0.02),
    }


def reference(x, positions, ev_w_in, ev_conv_w, ev_conv_b, ev_a_log, ev_dt_bias, ev_d_skip,
              ev_norm_g, ev_sc_conv_w, ev_w_out, ev_ln_g, ev_ln_b, od_w_in, od_q_norm_g, od_w_uq,
              od_kv_norm_g, od_w_ukv, od_pool_w, od_pool_scale, od_w_out, od_ln_g, od_ln_b):
    for layer in range(DEPTH):
        i = layer // 2
        if layer % 2 == 0:
            h = _even_layer(x, ev_w_in[i], ev_conv_w[i], ev_conv_b[i], ev_a_log[i], ev_dt_bias[i],
                            ev_d_skip[i], ev_norm_g[i], ev_sc_conv_w[i], ev_w_out[i])
            x = _layer_norm(ALPHA * x + h, ev_ln_g[i], ev_ln_b[i])
        else:
            h = _odd_layer(x, positions, od_w_in[i], od_q_norm_g[i], od_w_uq[i], od_kv_norm_g[i],
                           od_w_ukv[i], od_pool_w[i], od_pool_scale[i], od_w_out[i])
            x = _layer_norm(ALPHA * x + h, od_ln_g[i], od_ln_b[i])
    return x
```

```python
import functools
import math

import jax
import jax.numpy as jnp
from jax import lax
from jax.experimental import pallas as pl
from jax.experimental.pallas import tpu as pltpu

F32 = jnp.float32
BF16 = jnp.bfloat16

D_MODEL = 1024
DEPTH = 2
SSD_HEADS = 16
SSD_HEAD_DIM = 64
SSD_INNER = SSD_HEADS * SSD_HEAD_DIM
SSD_GROUPS = 4
SSD_STATE = 128
SSD_CONV = 4
SSD_CHUNK = 128
SSD_XBC = SSD_INNER + 2 * SSD_GROUPS * SSD_STATE
SC_WIDTH = 1024
MLA_HEADS = 8
MLA_Q_RANK = 256
MLA_KV_RANK = 128
MLA_NOPE = 64
MLA_ROPE = 32
MLA_V = 64
MLA_WIDTH = MLA_HEADS * MLA_V
ROPE_THETA = 10000.0
ATTN_SCALE = (MLA_NOPE + MLA_ROPE) ** -0.5
POOL_WINDOWS = (2, 4, 8, 16)
POOL_GROUP = 128
POOL_WIDTH = POOL_GROUP * len(POOL_WINDOWS)
EPS = 1e-5
ALPHA = (2 * DEPTH) ** 0.25

LANES = 128
HALO = 8
VMEM_LIMIT = 56 * 1024 * 1024


def _dot(a, b):
    return jnp.dot(a, b, preferred_element_type=F32)


def _silu(v):
    return v / (1.0 + jnp.exp(-v))


def _softplus(v):
    return jnp.maximum(v, 0.0) + jnp.log1p(jnp.exp(-jnp.abs(v)))


def _split_bf16(v, pieces):
    out = []
    r = v
    for _ in range(pieces):
        p = r.astype(BF16)
        out.append(p)
        r = r - p.astype(F32)
    return out


def _layer_norm_rows(r, g, b):
    mu = jnp.mean(r, axis=-1, keepdims=True)
    d = r - mu
    var = jnp.mean(d * d, axis=-1, keepdims=True)
    return d * lax.rsqrt(var + EPS) * g + b


def _rms_norm_rows(v, g):
    return v * lax.rsqrt(jnp.mean(v * v, axis=-1, keepdims=True) + EPS) * g


def _even_in_proj_kernel(x_ref, w_ref, wdh_ref, wdl_ref, o_ref, dt_ref, xb_ref):
    @pl.when(pl.program_id(1) == 0)
    def _():
        x = x_ref[...]
        xh = x.astype(BF16)
        xb_ref[...] = xh
        xl = (x - xh.astype(F32)).astype(BF16)
        wh = wdh_ref[...]
        dt_ref[...] = _dot(xh, wh) + _dot(xl, wh) + _dot(xh, wdl_ref[...])

    o_ref[...] = _dot(xb_ref[...], w_ref[...]).astype(BF16)


def _even_in_proj(x2d, w_main, wdt_hi, wdt_lo, tm, tn):
    t, k = x2d.shape
    n = w_main.shape[1]
    ndt = wdt_hi.shape[1]
    return pl.pallas_call(
        _even_in_proj_kernel,
        grid=(t // tm, n // tn),
        in_specs=[
            pl.BlockSpec((tm, k), lambda i, j: (i, 0)),
            pl.BlockSpec((k, tn), lambda i, j: (0, j)),
            pl.BlockSpec((k, ndt), lambda i, j: (0, 0)),
            pl.BlockSpec((k, ndt), lambda i, j: (0, 0)),
        ],
        out_specs=[
            pl.BlockSpec((tm, tn), lambda i, j: (i, j)),
            pl.BlockSpec((tm, ndt), lambda i, j: (i, 0)),
        ],
        out_shape=[jax.ShapeDtypeStruct((t, n), BF16), jax.ShapeDtypeStruct((t, ndt), F32)],
        scratch_shapes=[pltpu.VMEM((tm, k), BF16)],
        compiler_params=pltpu.CompilerParams(
            dimension_semantics=("parallel", "arbitrary"), vmem_limit_bytes=VMEM_LIMIT),
        name="even_in_proj",
    )(x2d, w_main, wdt_hi, wdt_lo)


def _ssd_chunk(reverse, xm_ref, xp_ref, xn_ref, dt_ref, cw_ref, cb_ref, a_ref, dtb_ref, dsk_ref,
               e_ref, ext_ref, h_ref, emit):
    q = SSD_CHUNK
    c = pl.program_id(1)
    nc = pl.num_programs(1)
    chunk = (nc - 1 - c) if reverse else c

    @pl.when(c == 0)
    def _():
        h_ref[...] = jnp.zeros_like(h_ref)

    ext_ref[HALO:HALO + q, :] = xm_ref[...].astype(F32)
    ext_ref[0:HALO, :] = jnp.where(chunk > 0, xp_ref[...].astype(F32), 0.0)
    ext_ref[HALO + q:2 * HALO + q, :] = jnp.where(chunk < nc - 1, xn_ref[...].astype(F32), 0.0)
    acc = cb_ref[...] + cw_ref[0:1, :] * ext_ref[HALO - 2:HALO - 2 + q, :]
    for k in range(1, SSD_CONV):
        acc = acc + cw_ref[k:k + 1, :] * ext_ref[HALO - 2 + k:HALO - 2 + k + q, :]
    xbc = _silu(acc)
    xs = xbc[:, :SSD_INNER]
    bm = xbc[:, SSD_INNER:SSD_INNER + SSD_GROUPS * SSD_STATE]
    cm = xbc[:, SSD_INNER + SSD_GROUPS * SSD_STATE:]

    dt = _softplus(dt_ref[...] + dtb_ref[...])
    da = dt * a_ref[...]
    row = lax.broadcasted_iota(jnp.int32, (q, q), 0)
    col = lax.broadcasted_iota(jnp.int32, (q, q), 1)
    keep = (col >= row) if reverse else (col <= row)
    tri = jnp.where(keep, 1.0, 0.0).astype(BF16)
    p1, p2, p3 = _split_bf16(da, 3)
    cs = _dot(tri, p1) + _dot(tri, p2) + _dot(tri, p3)
    cs_t = cs.T
    end = 0 if reverse else q - 1
    ecs = jnp.exp(cs)
    dec = jnp.exp(cs[end:end + 1, :] - cs)

    lane = lax.broadcasted_iota(jnp.int32, (q, LANES), 1)

    def expand(v):
        hi, lo = _split_bf16(v, 2)
        return _dot(jnp.where(lane < SSD_HEADS, hi, lo), e_ref[...])

    dt_full = expand(dt)
    ecs_full = expand(ecs)
    dec_full = expand(dec)
    xd = xs * dt_full
    xd_b = xd.astype(BF16)
    xdd_b = (xd * dec_full).astype(BF16)
    ecs_end = ecs_full[end:end + 1, :]
    half = lane < SSD_HEAD_DIM
    rep = SSD_HEADS // SSD_GROUPS
    gw = rep * SSD_HEAD_DIM

    for g in range(SSD_GROUPS):
        bg = bm[:, g * SSD_STATE:(g + 1) * SSD_STATE]
        cg_b = cm[:, g * SSD_STATE:(g + 1) * SSD_STATE].astype(BF16)
        bgt_b = bg.T.astype(BF16)
        cbm = _dot(cg_b, bgt_b)
        h_in = h_ref[g]
        y_off = _dot(cg_b, h_in.astype(BF16))
        st = _dot(bgt_b, xdd_b[:, g * gw:(g + 1) * gw])
        h_ref[g] = h_in * ecs_end[:, g * gw:(g + 1) * gw] + st
        for jj in range(gw // LANES):
            cidx = g * gw + jj * LANES
            xd_blk = xd_b[:, cidx:cidx + LANES]
            yd = None
            for hh in range(2):
                h = (cidx // SSD_HEAD_DIM) + hh
                diff = cs[:, h:h + 1] - cs_t[h:h + 1, :]
                m = (cbm * jnp.where(keep, jnp.exp(diff), 0.0)).astype(BF16)
                rhs = jnp.where(half if hh == 0 else jnp.logical_not(half), xd_blk, jnp.zeros_like(xd_blk))
                t = _dot(m, rhs)
                yd = t if yd is None else yd + t
            xs_blk = xs[:, cidx:cidx + LANES]
            y_blk = (yd + y_off[:, jj * LANES:(jj + 1) * LANES] * ecs_full[:, cidx:cidx + LANES]
                     + dsk_ref[:, cidx:cidx + LANES] * xs_blk)
            emit(cidx, y_blk)


def _ssd_bwd_kernel(xm_ref, xp_ref, xn_ref, dt_ref, cw_ref, cb_ref, a_ref, dtb_ref, dsk_ref, e_ref,
                    y_ref, ext_ref, h_ref):
    def emit(cidx, y_blk):
        y_ref[:, cidx:cidx + LANES] = y_blk

    _ssd_chunk(True, xm_ref, xp_ref, xn_ref, dt_ref, cw_ref, cb_ref, a_ref, dtb_ref, dsk_ref, e_ref,
               ext_ref, h_ref, emit)


def _ssd_fwd_kernel(xm_ref, xp_ref, xn_ref, dt_ref, cw_ref, cb_ref, a_ref, dtb_ref, dsk_ref, e_ref,
                    z_ref, yb_ref, ng_ref, y_ref, ext_ref, h_ref, yz_ref):
    def emit(cidx, y_blk):
        z = z_ref[:, cidx:cidx + LANES].astype(F32)
        yz_ref[:, cidx:cidx + LANES] = (y_blk + yb_ref[:, cidx:cidx + LANES]) * _silu(z)

    _ssd_chunk(False, xm_ref, xp_ref, xn_ref, dt_ref, cw_ref, cb_ref, a_ref, dtb_ref, dsk_ref, e_ref,
               ext_ref, h_ref, emit)
    y_ref[...] = _rms_norm_rows(yz_ref[...], ng_ref[...]).astype(BF16)


def _ssd_sweep(reverse, proj, dt_raw, conv_w, conv_b, a_row, dtb_row, dsk_row, e_mat, s,
               extra_inputs=(), norm_g=None):
    t = proj.shape[0]
    b = t // s
    q = SSD_CHUNK
    nc = s // q
    hb = q // HALO
    nhb = s // HALO
    dcol = 1 if reverse else 0

    def ch(c):
        return (nc - 1 - c) if reverse else c

    main = lambda bi, c: (bi * nc + ch(c), 0)
    prev = lambda bi, c: (bi * nhb + jnp.maximum(ch(c) * hb - 1, 0), 0)
    nxt = lambda bi, c: (bi * nhb + jnp.minimum((ch(c) + 1) * hb, nhb - 1), 0)
    const = lambda bi, c: (0, 0)
    in_specs = [
        pl.BlockSpec((q, SSD_XBC), main),
        pl.BlockSpec((HALO, SSD_XBC), prev),
        pl.BlockSpec((HALO, SSD_XBC), nxt),
        pl.BlockSpec((q, LANES), lambda bi, c: (bi * nc + ch(c), dcol)),
        pl.BlockSpec((SSD_CONV, SSD_XBC), const),
        pl.BlockSpec((1, SSD_XBC), const),
        pl.BlockSpec((1, LANES), const),
        pl.BlockSpec((1, LANES), const),
        pl.BlockSpec((1, SSD_INNER), const),
        pl.BlockSpec((LANES, SSD_INNER), const),
    ]
    args = [proj, proj, proj, dt_raw, conv_w, conv_b, a_row, dtb_row, dsk_row, e_mat]
    scratch = [pltpu.VMEM((q + 2 * HALO, SSD_XBC), F32),
               pltpu.VMEM((SSD_GROUPS, SSD_STATE, SSD_INNER // SSD_GROUPS), F32)]
    if reverse:
        kern = _ssd_bwd_kernel
        out_dtype = F32
        name = "ssd_backward_sweep"
    else:
        kern = _ssd_fwd_kernel
        z_all, yb = extra_inputs
        in_specs += [
            pl.BlockSpec((q, SSD_INNER), lambda bi, c: (bi * nc + c, 2)),
            pl.BlockSpec((q, SSD_INNER), main),
            pl.BlockSpec((1, SSD_INNER), const),
        ]
        args += [z_all, yb, norm_g]
        scratch.append(pltpu.VMEM((q, SSD_INNER), F32))
        out_dtype = BF16
        name = "ssd_forward_sweep"
    return pl.pallas_call(
        kern,
        grid=(b, nc),
        in_specs=in_specs,
        out_specs=pl.BlockSpec((q, SSD_INNER), main),
        out_shape=jax.ShapeDtypeStruct((t, SSD_INNER), out_dtype),
        scratch_shapes=scratch,
        compiler_params=pltpu.CompilerParams(
            dimension_semantics=("parallel", "arbitrary"), vmem_limit_bytes=VMEM_LIMIT),
        name=name,
    )(*args)


def _even_out_kernel(ya_ref, bg_ref, cgm_ref, cgp_ref, cgn_ref, hm_ref, hp_ref, hn_ref, gt_ref, x_ref,
                     scw_ref, wa_ref, wb_ref, lg_ref, lb_ref, o_ref, ext_ref, *, tiles_per_seq):
    tm = ya_ref.shape[0]
    i = pl.program_id(0) % tiles_per_seq
    ext_ref[HALO:HALO + tm, :] = cgm_ref[...].astype(F32) * hm_ref[...].astype(F32)
    ext_ref[0:HALO, :] = jnp.where(i > 0, cgp_ref[...].astype(F32) * hp_ref[...].astype(F32), 0.0)
    ext_ref[HALO + tm:2 * HALO + tm, :] = jnp.where(
        i < tiles_per_seq - 1, cgn_ref[...].astype(F32) * hn_ref[...].astype(F32), 0.0)
    conv = scw_ref[0:1, :] * ext_ref[HALO - 1:HALO - 1 + tm, :]
    conv = conv + scw_ref[1:2, :] * ext_ref[HALO:HALO + tm, :]
    conv = conv + scw_ref[2:3, :] * ext_ref[HALO + 1:HALO + 1 + tm, :]
    y_b = bg_ref[...].astype(F32) * conv * _silu(gt_ref[...].astype(F32))
    h = _dot(ya_ref[...], wa_ref[...]) + _dot(y_b.astype(BF16), wb_ref[...])
    o_ref[...] = _layer_norm_rows(ALPHA * x_ref[...] + h, lg_ref[...], lb_ref[...])


def _even_out(ya, proj, x2d, sc_w, w_a, w_b, ln_g, ln_b, s, tm):
    t = x2d.shape[0]
    tps = s // tm
    hb = tm // HALO
    nh = t // HALO

    def colblk(j):
        return lambda i: (i, j)

    def prev(j):
        return lambda i: (jnp.maximum(i * hb - 1, 0), j)

    def nxt(j):
        return lambda i: (jnp.minimum((i + 1) * hb, nh - 1), j)

    w = SC_WIDTH
    const = lambda i: (0, 0)
    return pl.pallas_call(
        functools.partial(_even_out_kernel, tiles_per_seq=tps),
        grid=(t // tm,),
        in_specs=[
            pl.BlockSpec((tm, w), colblk(0)),
            pl.BlockSpec((tm, w), colblk(3)),
            pl.BlockSpec((tm, w), colblk(4)), pl.BlockSpec((HALO, w), prev(4)), pl.BlockSpec((HALO, w), nxt(4)),
            pl.BlockSpec((tm, w), colblk(5)), pl.BlockSpec((HALO, w), prev(5)), pl.BlockSpec((HALO, w), nxt(5)),
            pl.BlockSpec((tm, w), colblk(6)),
            pl.BlockSpec((tm, D_MODEL), colblk(0)),
            pl.BlockSpec((3, w), const),
            pl.BlockSpec((SSD_INNER, D_MODEL), const),
            pl.BlockSpec((w, D_MODEL), const),
            pl.BlockSpec((1, D_MODEL), const),
            pl.BlockSpec((1, D_MODEL), const),
        ],
        out_specs=pl.BlockSpec((tm, D_MODEL), colblk(0)),
        out_shape=jax.ShapeDtypeStruct((t, D_MODEL), F32),
        scratch_shapes=[pltpu.VMEM((tm + 2 * HALO, w), F32)],
        compiler_params=pltpu.CompilerParams(
            dimension_semantics=("parallel",), vmem_limit_bytes=VMEM_LIMIT),
        name="even_out_proj",
    )(ya, proj, proj, proj, proj, proj, proj, proj, proj, x2d, sc_w, w_a, w_b, ln_g, ln_b)


_O_CQ = 0
_O_CKV = MLA_Q_RANK
_O_KRA = _O_CKV + MLA_KV_RANK
_O_KRB = _O_KRA + LANES
_O_REST = _O_KRB + LANES
_ODD_REST = MLA_WIDTH + 2 * POOL_WIDTH
_ODD_COLS = _O_REST + _ODD_REST
_HEAD_W = LANES


def _odd_in_kernel(x_ref, pos_ref, w_ref, qg_ref, kg_ref, wq_ref, wqs_ref, wkn_ref, wv_ref, frq_ref,
                   sgn_ref, vone_ref, q_ref, k_ref, v_ref, r_ref):
    proj = _dot(x_ref[...].astype(BF16), w_ref[...])
    r_ref[...] = proj[:, _O_REST:].astype(BF16)
    ang = pos_ref[...].astype(F32) * frq_ref[...]
    cosf = jnp.cos(ang)
    sinf = jnp.sin(ang) * sgn_ref[...]
    cqn = _rms_norm_rows(proj[:, _O_CQ:_O_CQ + MLA_Q_RANK], qg_ref[...]).astype(BF16)
    qa = _dot(cqn, wq_ref[...])
    qb = _dot(cqn, wqs_ref[...])
    ckvn = _rms_norm_rows(proj[:, _O_CKV:_O_CKV + MLA_KV_RANK], kg_ref[...]).astype(BF16)
    kn = _dot(ckvn, wkn_ref[...])
    vv = _dot(ckvn, wv_ref[...]) + vone_ref[...]
    kr = proj[:, _O_KRA:_O_KRA + LANES] * cosf + proj[:, _O_KRB:_O_KRB + LANES] * sinf
    for h in range(MLA_HEADS):
        sl = slice(h * _HEAD_W, (h + 1) * _HEAD_W)
        q_ref[h] = ((qa[:, sl] * cosf + qb[:, sl] * sinf) * ATTN_SCALE).astype(BF16)
        k_ref[h] = (kn[:, sl] + kr).astype(BF16)
        v_ref[h] = vv[:, sl].astype(BF16)


def _odd_in(x2d, pos2d, w2, qg, kg, wq, wqs, wkn, wv, frq, sgn, vone, s, tm):
    t = x2d.shape[0]
    b = t // s
    tps = s // tm
    const = lambda i: (0, 0)
    hw = MLA_HEADS * _HEAD_W
    qkv_spec = pl.BlockSpec((None, MLA_HEADS, tm, _HEAD_W), lambda i: (i // tps, 0, i % tps, 0))
    qkv_shape = jax.ShapeDtypeStruct((b, MLA_HEADS, s, _HEAD_W), BF16)
    return pl.pallas_call(
        _odd_in_kernel,
        grid=(t // tm,),
        in_specs=[
            pl.BlockSpec((tm, D_MODEL), lambda i: (i, 0)),
            pl.BlockSpec((tm, 1), lambda i: (i, 0)),
            pl.BlockSpec((D_MODEL, _ODD_COLS), const),
            pl.BlockSpec((1, MLA_Q_RANK), const),
            pl.BlockSpec((1, MLA_KV_RANK), const),
            pl.BlockSpec((MLA_Q_RANK, hw), const),
            pl.BlockSpec((MLA_Q_RANK, hw), const),
            pl.BlockSpec((MLA_KV_RANK, hw), const),
            pl.BlockSpec((MLA_KV_RANK, hw), const),
            pl.BlockSpec((1, LANES), const),
            pl.BlockSpec((1, LANES), const),
            pl.BlockSpec((1, hw), const),
        ],
        out_specs=[qkv_spec, qkv_spec, qkv_spec, pl.BlockSpec((tm, _ODD_REST), lambda i: (i, 0))],
        out_shape=[qkv_shape, qkv_shape, qkv_shape, jax.ShapeDtypeStruct((t, _ODD_REST), BF16)],
        compiler_params=pltpu.CompilerParams(
            dimension_semantics=("parallel",), vmem_limit_bytes=VMEM_LIMIT),
        name="odd_in_proj_qkv",
    )(x2d, pos2d, w2, qg, kg, wq, wqs, wkn, wv, frq, sgn, vone)


def _attn_kernel(q_ref, k_ref, v_ref, o_ref, *, tk):
    tq = q_ref.shape[0]
    nk = k_ref.shape[0] // tk
    q = q_ref[...]

    def body(i, carry):
        m, acc = carry
        off = pl.multiple_of(i * tk, tk)
        k = k_ref[pl.ds(off, tk), :]
        v = v_ref[pl.ds(off, tk), :]
        sc = lax.dot_general(q, k, (((1,), (1,)), ((), ())), preferred_element_type=F32)
        m_new = jnp.maximum(m, jnp.max(sc, axis=1, keepdims=True))
        p = jnp.exp(sc - m_new).astype(BF16)
        acc = acc * jnp.exp(m - m_new) + _dot(p, v)
        return m_new, acc

    m0 = jnp.full((tq, 1), -jnp.inf, F32)
    acc0 = jnp.zeros((tq, _HEAD_W), F32)
    _, acc = lax.fori_loop(0, nk, body, (m0, acc0))
    odd = (pl.program_id(1) % 2) == 1
    denom = jnp.where(odd, acc[:, 0:1], acc[:, MLA_V:MLA_V + 1])
    o_ref[...] = (acc / denom).astype(BF16)


def _attention(q, k, v, tq, tk):
    b, h, s, w = q.shape
    return pl.pallas_call(
        functools.partial(_attn_kernel, tk=tk),
        grid=(b, h, s // tq),
        in_specs=[
            pl.BlockSpec((None, None, tq, w), lambda bi, hi, qi: (bi, hi, qi, 0)),
            pl.BlockSpec((None, None, s, w), lambda bi, hi, qi: (bi, hi, 0, 0)),
            pl.BlockSpec((None, None, s, w), lambda bi, hi, qi: (bi, hi, 0, 0)),
        ],
        out_specs=pl.BlockSpec((None, None, tq, w), lambda bi, hi, qi: (bi, hi, qi, 0)),
        out_shape=jax.ShapeDtypeStruct((b, h, s, w), BF16),
        compiler_params=pltpu.CompilerParams(
            dimension_semantics=("parallel", "parallel", "arbitrary"), vmem_limit_bytes=VMEM_LIMIT),
        name="mla_attention",
    )(q, k, v)


def _odd_out_kernel(o_ref, gc_ref, um_ref, up_ref, un_ref, gd_ref, x_ref, pw_ref, ps_ref, wc_ref, wd_ref,
                    lg_ref, lb_ref, out_ref, ext_ref, *, tiles_per_seq, seq):
    tm = x_ref.shape[0]
    i = pl.program_id(0) % tiles_per_seq
    lane = lax.broadcasted_iota(jnp.int32, (tm, LANES), 1)
    low = lane < MLA_V
    gc = _silu(gc_ref[...].astype(F32))
    acc = None
    for j in range(MLA_HEADS // 2):
        pair = jnp.where(low, o_ref[2 * j], o_ref[2 * j + 1]).astype(F32)
        yc = (pair * gc[:, j * LANES:(j + 1) * LANES]).astype(BF16)
        t = _dot(yc, wc_ref[j * LANES:(j + 1) * LANES, :])
        acc = t if acc is None else acc + t

    um = um_ref[...].astype(F32)
    ext_ref[HALO:HALO + tm, :] = um
    ext_ref[0:HALO, :] = jnp.where(i > 0, up_ref[...].astype(F32), 0.0)
    ext_ref[HALO + tm:2 * HALO + tm, :] = jnp.where(i < tiles_per_seq - 1, un_ref[...].astype(F32), 0.0)
    pos = i * tm + lax.broadcasted_iota(jnp.int32, (tm, LANES), 0)
    gd = _silu(gd_ref[...].astype(F32))
    for gi, w in enumerate(POOL_WINDOWS):
        cs = slice(gi * POOL_GROUP, (gi + 1) * POOL_GROUP)
        wsum = None
        for d in range(-(w // 2), w - w // 2):
            term = ext_ref[HALO + d:HALO + d + tm, cs]
            wsum = term if wsum is None else wsum + term
        cnt = (jnp.minimum(pos + (w - w // 2), seq) - jnp.maximum(pos - w // 2, 0)).astype(F32)
        pooled = wsum / cnt - um[:, cs]
        yd = _dot(pooled.astype(BF16), pw_ref[gi]) * ps_ref[:, cs] * gd[:, cs]
        acc = acc + _dot(yd.astype(BF16), wd_ref[cs, :])
    out_ref[...] = _layer_norm_rows(ALPHA * x_ref[...] + acc, lg_ref[...], lb_ref[...])


def _odd_out(o, rest, x2d, pool_w, pool_scale, w_c, w_d, ln_g, ln_b, s, tm):
    t = x2d.shape[0]
    tps = s // tm
    hb = tm // HALO
    nh = t // HALO
    const = lambda i: (0, 0)
    pw = POOL_WIDTH
    return pl.pallas_call(
        functools.partial(_odd_out_kernel, tiles_per_seq=tps, seq=s),
        grid=(t // tm,),
        in_specs=[
            pl.BlockSpec((None, MLA_HEADS, tm, _HEAD_W), lambda i: (i // tps, 0, i % tps, 0)),
            pl.BlockSpec((tm, pw), lambda i: (i, 0)),
            pl.BlockSpec((tm, pw), lambda i: (i, 1)),
            pl.BlockSpec((HALO, pw), lambda i: (jnp.maximum(i * hb - 1, 0), 1)),
            pl.BlockSpec((HALO, pw), lambda i: (jnp.minimum((i + 1) * hb, nh - 1), 1)),
            pl.BlockSpec((tm, pw), lambda i: (i, 2)),
            pl.BlockSpec((tm, D_MODEL), lambda i: (i, 0)),
            pl.BlockSpec((len(POOL_WINDOWS), POOL_GROUP, POOL_GROUP), lambda i: (0, 0, 0)),
            pl.BlockSpec((1, pw), const),
            pl.BlockSpec((MLA_WIDTH, D_MODEL), const),
            pl.BlockSpec((pw, D_MODEL), const),
            pl.BlockSpec((1, D_MODEL), const),
            pl.BlockSpec((1, D_MODEL), const),
        ],
        out_specs=pl.BlockSpec((tm, D_MODEL), lambda i: (i, 0)),
        out_shape=jax.ShapeDtypeStruct((t, D_MODEL), F32),
        scratch_shapes=[pltpu.VMEM((tm + 2 * HALO, pw), F32)],
        compiler_params=pltpu.CompilerParams(
            dimension_semantics=("parallel",), vmem_limit_bytes=VMEM_LIMIT),
        name="odd_out_proj",
    )(o, rest, rest, rest, rest, rest, x2d, pool_w, pool_scale, w_c, w_d, ln_g, ln_b)


def _rep3(v):
    z = jnp.zeros((LANES - 3 * SSD_HEADS,), F32)
    return jnp.concatenate([v, v, v, z])[None, :]


def _even_params(w_in, conv_w, conv_b, a_log, dt_bias, d_skip, norm_g, sc_conv_w, w_out):
    o_z = 0
    o_xbc = SSD_INNER
    o_dt = o_xbc + SSD_XBC
    o_sc = o_dt + 2 * SSD_HEADS
    w_main = jnp.concatenate([w_in[:, o_xbc:o_dt], w_in[:, o_z:o_xbc], w_in[:, o_sc:]], axis=1).astype(BF16)
    zpad = jnp.zeros((D_MODEL, LANES - 3 * SSD_HEADS), F32)
    wdt = []
    for k in range(2):
        wk = w_in[:, o_dt + k * SSD_HEADS:o_dt + (k + 1) * SSD_HEADS]
        wdt.append(jnp.concatenate([wk, wk, wk, zpad], axis=1))
    wdt = jnp.concatenate(wdt, axis=1)
    wdt_hi = wdt.astype(BF16)
    wdt_lo = (wdt - wdt_hi.astype(F32)).astype(BF16)
    a_rows = [_rep3(-jnp.exp(a_log[k].astype(F32))) for k in range(2)]
    dtb_rows = [_rep3(dt_bias[k].astype(F32)) for k in range(2)]
    dsk_rows = [jnp.repeat(d_skip[k].astype(F32), SSD_HEAD_DIM)[None, :] for k in range(2)]
    src = jnp.arange(LANES)[:, None]
    dst = jnp.arange(SSD_INNER)[None, :] // SSD_HEAD_DIM
    e_mat = jnp.where((src < 2 * SSD_HEADS) & (src % SSD_HEADS == dst), 1.0, 0.0).astype(BF16)
    return dict(
        w_main=w_main, wdt_hi=wdt_hi, wdt_lo=wdt_lo, conv_w=conv_w.astype(F32), conv_b=conv_b[None, :].astype(F32),
        a_rows=a_rows, dtb_rows=dtb_rows, dsk_rows=dsk_rows, e_mat=e_mat, norm_g=norm_g[None, :].astype(F32),
        sc_w=sc_conv_w.astype(F32), w_a=w_out[:SSD_INNER].astype(BF16), w_b=w_out[SSD_INNER:].astype(BF16))


def _odd_params(w_in, q_norm_g, w_uq, kv_norm_g, w_ukv, pool_w, pool_scale, w_out):
    o_ckv = MLA_Q_RANK
    o_kr = o_ckv + MLA_KV_RANK
    o_rest = o_kr + MLA_ROPE
    half = MLA_ROPE // 2
    z64 = jnp.zeros((D_MODEL, MLA_NOPE), F32)
    z32 = jnp.zeros((D_MODEL, LANES - MLA_NOPE - MLA_ROPE), F32)
    kr1 = w_in[:, o_kr:o_kr + half]
    kr2 = w_in[:, o_kr + half:o_kr + MLA_ROPE]
    w2 = jnp.concatenate([w_in[:, :o_kr], z64, kr1, kr2, z32, z64, kr2, kr1, z32, w_in[:, o_rest:]],
                         axis=1).astype(BF16)
    qd = MLA_NOPE + MLA_ROPE
    w3 = w_uq.reshape(MLA_Q_RANK, MLA_HEADS, qd)
    zq = lambda n: jnp.zeros((MLA_Q_RANK, MLA_HEADS, n), F32)
    wq = jnp.concatenate([w3, zq(_HEAD_W - qd)], axis=-1).reshape(MLA_Q_RANK, -1).astype(BF16)
    wqs = jnp.concatenate([zq(MLA_NOPE), w3[..., MLA_NOPE + half:], w3[..., MLA_NOPE:MLA_NOPE + half],
                           zq(_HEAD_W - qd)], axis=-1).reshape(MLA_Q_RANK, -1).astype(BF16)
    w4 = w_ukv.reshape(MLA_KV_RANK, MLA_HEADS, MLA_NOPE + MLA_V)
    zk = jnp.zeros((MLA_KV_RANK, MLA_HEADS, _HEAD_W - MLA_NOPE), F32)
    wkn = jnp.concatenate([w4[..., :MLA_NOPE], zk], axis=-1).reshape(MLA_KV_RANK, -1).astype(BF16)
    zv = jnp.zeros((MLA_KV_RANK, _HEAD_W - MLA_V), F32)
    wv_blocks = []
    vone = []
    for h in range(MLA_HEADS):
        vh = w4[:, h, MLA_NOPE:]
        wv_blocks.append(jnp.concatenate([vh, zv] if h % 2 == 0 else [zv, vh], axis=-1))
        vone.append(jnp.zeros((_HEAD_W,), F32).at[MLA_V if h % 2 == 0 else 0].set(1.0))
    wv = jnp.concatenate(wv_blocks, axis=-1).astype(BF16)
    vone = jnp.concatenate(vone)[None, :]
    inv_freq = ROPE_THETA ** (-jnp.arange(half, dtype=F32) / half)
    zl = lambda n: jnp.zeros((n,), F32)
    frq = jnp.concatenate([zl(MLA_NOPE), inv_freq, inv_freq, zl(LANES - qd)])[None, :]
    sgn = jnp.concatenate([zl(MLA_NOPE), -jnp.ones((half,), F32), jnp.ones((half,), F32), zl(LANES - qd)])[None, :]
    return dict(
        w2=w2, qg=q_norm_g[None, :].astype(F32), kg=kv_norm_g[None, :].astype(F32), wq=wq, wqs=wqs, wkn=wkn,
        wv=wv, vone=vone, frq=frq, sgn=sgn, pool_w=pool_w.astype(BF16), pool_scale=pool_scale[None, :].astype(F32),
        w_c=w_out[:MLA_WIDTH].astype(BF16), w_d=w_out[MLA_WIDTH:].astype(BF16))


def _pick(n, pref):
    t = min(pref, n)
    while n % t:
        t //= 2
    return t


def _even_layer(x2d, s, p, ln_g, ln_b):
    t = x2d.shape[0]
    proj, dt_raw = _even_in_proj(x2d, p["w_main"], p["wdt_hi"], p["wdt_lo"], _pick(t, 1024), 1024)
    sweep = functools.partial(_ssd_sweep, proj=proj, dt_raw=dt_raw, conv_w=p["conv_w"], conv_b=p["conv_b"],
                              e_mat=p["e_mat"], s=s)
    yb = sweep(True, a_row=p["a_rows"][1], dtb_row=p["dtb_rows"][1], dsk_row=p["dsk_rows"][1])
    ya = sweep(False, a_row=p["a_rows"][0], dtb_row=p["dtb_rows"][0], dsk_row=p["dsk_rows"][0],
               extra_inputs=(proj, yb), norm_g=p["norm_g"])
    return _even_out(ya, proj, x2d, p["sc_w"], p["w_a"], p["w_b"], ln_g[None, :], ln_b[None, :], s, _pick(s, 512))


def _odd_layer(x2d, pos2d, s, p, ln_g, ln_b):
    tm = _pick(s, 512)
    q, k, v, rest = _odd_in(x2d, pos2d, p["w2"], p["qg"], p["kg"], p["wq"], p["wqs"], p["wkn"], p["wv"],
                            p["frq"], p["sgn"], p["vone"], s, tm)
    o = _attention(q, k, v, _pick(s, 512), _pick(s, 512))
    return _odd_out(o, rest, x2d, p["pool_w"], p["pool_scale"], p["w_c"], p["w_d"], ln_g[None, :], ln_b[None, :],
                    s, tm)


def kernel(x, positions, ev_w_in, ev_conv_w, ev_conv_b, ev_a_log, ev_dt_bias, ev_d_skip, ev_norm_g, ev_sc_conv_w,
           ev_w_out, ev_ln_g, ev_ln_b, od_w_in, od_q_norm_g, od_w_uq, od_kv_norm_g, od_w_ukv, od_pool_w,
           od_pool_scale, od_w_out, od_ln_g, od_ln_b):
    b, s, d = x.shape
    x2d = x.reshape(b * s, d)
    pos2d = positions.reshape(b * s, 1)
    for layer in range(DEPTH):
        i = layer // 2
        if layer % 2 == 0:
            p = _even_params(ev_w_in[i], ev_conv_w[i], ev_conv_b[i], ev_a_log[i], ev_dt_bias[i], ev_d_skip[i],
                             ev_norm_g[i], ev_sc_conv_w[i], ev_w_out[i])
            x2d = _even_layer(x2d, s, p, ev_ln_g[i], ev_ln_b[i])
        else:
            p = _odd_params(od_w_in[i], od_q_norm_g[i], od_w_uq[i], od_kv_norm_g[i], od_w_ukv[i], od_pool_w[i],
                            od_pool_scale[i], od_w_out[i])
            x2d = _odd_layer(x2d, pos2d, s, p, od_ln_g[i], od_ln_b[i])
    return x2d.reshape(b, s, d)
```

```python
import functools
import math

import jax
import jax.numpy as jnp
from jax import lax
from jax.experimental import pallas as pl
from jax.experimental.pallas import tpu as pltpu

F32 = jnp.float32
BF16 = jnp.bfloat16

D_MODEL = 1024
DEPTH = 2
SSD_HEADS = 16
SSD_HEAD_DIM = 64
SSD_INNER = SSD_HEADS * SSD_HEAD_DIM
SSD_GROUPS = 4
SSD_STATE = 128
SSD_CONV = 4
SSD_CHUNK = 128
SSD_XBC = SSD_INNER + 2 * SSD_GROUPS * SSD_STATE
SC_WIDTH = 1024
MLA_HEADS = 8
MLA_Q_RANK = 256
MLA_KV_RANK = 128
MLA_NOPE = 64
MLA_ROPE = 32
MLA_V = 64
MLA_WIDTH = MLA_HEADS * MLA_V
ROPE_THETA = 10000.0
ATTN_SCALE = (MLA_NOPE + MLA_ROPE) ** -0.5
LOG2E = math.log2(math.e)
POOL_WINDOWS = (2, 4, 8, 16)
POOL_GROUP = 128
POOL_WIDTH = POOL_GROUP * len(POOL_WINDOWS)
EPS = 1e-5
ALPHA = (2 * DEPTH) ** 0.25

LANES = 128
HALO = 8
VMEM_LIMIT = 56 * 1024 * 1024


def _dot(a, b):
    return jnp.dot(a, b, preferred_element_type=F32)


def _silu(v):
    h = 0.5 * v
    return h + h * jnp.tanh(h)


def _softplus(v):
    return jnp.maximum(v, 0.0) + jnp.log1p(jnp.exp(-jnp.abs(v)))


def _split_bf16(v, pieces):
    out = []
    r = v
    for _ in range(pieces):
        p = r.astype(BF16)
        out.append(p)
        r = r - p.astype(F32)
    return out


def _layer_norm_rows(r, g, b):
    mu = jnp.mean(r, axis=-1, keepdims=True)
    d = r - mu
    var = jnp.mean(d * d, axis=-1, keepdims=True)
    return d * lax.rsqrt(var + EPS) * g + b


def _rms_norm_rows(v, g):
    return v * lax.rsqrt(jnp.mean(v * v, axis=-1, keepdims=True) + EPS) * g


def _even_in_proj_kernel(x_ref, w_ref, wdh_ref, wdl_ref, o_ref, dt_ref, xb_ref):
    @pl.when(pl.program_id(1) == 0)
    def _():
        x = x_ref[...]
        xh = x.astype(BF16)
        xb_ref[...] = xh
        xl = (x - xh.astype(F32)).astype(BF16)
        wh = wdh_ref[...]
        dt_ref[...] = _dot(xh, wh) + _dot(xl, wh) + _dot(xh, wdl_ref[...])

    o_ref[...] = _dot(xb_ref[...], w_ref[...]).astype(BF16)


def _even_in_proj(x2d, w_main, wdt_hi, wdt_lo, tm, tn):
    t, k = x2d.shape
    n = w_main.shape[1]
    ndt = wdt_hi.shape[1]
    return pl.pallas_call(
        _even_in_proj_kernel,
        grid=(t // tm, n // tn),
        in_specs=[
            pl.BlockSpec((tm, k), lambda i, j: (i, 0)),
            pl.BlockSpec((k, tn), lambda i, j: (0, j)),
            pl.BlockSpec((k, ndt), lambda i, j: (0, 0)),
            pl.BlockSpec((k, ndt), lambda i, j: (0, 0)),
        ],
        out_specs=[
            pl.BlockSpec((tm, tn), lambda i, j: (i, j)),
            pl.BlockSpec((tm, ndt), lambda i, j: (i, 0)),
        ],
        out_shape=[jax.ShapeDtypeStruct((t, n), BF16), jax.ShapeDtypeStruct((t, ndt), F32)],
        scratch_shapes=[pltpu.VMEM((tm, k), BF16)],
        compiler_params=pltpu.CompilerParams(
            dimension_semantics=("parallel", "arbitrary"), vmem_limit_bytes=VMEM_LIMIT),
        name="even_in_proj",
    )(x2d, w_main, wdt_hi, wdt_lo)


def _conv_silu(chunk, nc, xm_ref, xp_ref, xn_ref, cw_ref, cb_ref, ext_ref, xact_ref):
    q = SSD_CHUNK
    blk = 2 * LANES
    for j in range(SSD_XBC // blk):
        sl = slice(j * blk, (j + 1) * blk)
        ext_ref[HALO:HALO + q, sl] = xm_ref[:, sl].astype(F32)
        ext_ref[0:HALO, sl] = jnp.where(chunk > 0, xp_ref[:, sl].astype(F32), 0.0)
        ext_ref[HALO + q:2 * HALO + q, sl] = jnp.where(chunk < nc - 1, xn_ref[:, sl].astype(F32), 0.0)
        acc = cb_ref[:, sl] + cw_ref[0:1, sl] * ext_ref[HALO - 2:HALO - 2 + q, sl]
        for k in range(1, SSD_CONV):
            acc = acc + cw_ref[k:k + 1, sl] * ext_ref[HALO - 2 + k:HALO - 2 + k + q, sl]
        xact_ref[:, sl] = _silu(acc).astype(BF16)


def _ssd_chunk(reverse, xact_ref, dt_ref, a_ref, dtb_ref, dsk_ref, e_ref, h_ref, emit):
    q = SSD_CHUNK

    @pl.when(pl.program_id(1) == 0)
    def _():
        h_ref[...] = jnp.zeros_like(h_ref)

    dt = _softplus(dt_ref[...] + dtb_ref[...])
    da = dt * a_ref[...]
    row = lax.broadcasted_iota(jnp.int32, (q, q), 0)
    col = lax.broadcasted_iota(jnp.int32, (q, q), 1)
    keep = (col >= row) if reverse else (col <= row)
    tri = jnp.where(keep, 1.0, 0.0).astype(BF16)
    p1, p2, p3 = _split_bf16(da, 3)
    cs = _dot(tri, p1) + _dot(tri, p2) + _dot(tri, p3)
    cs_t = cs.T
    end = 0 if reverse else q - 1
    ecs = jnp.exp2(cs)
    dec = jnp.exp2(cs[end:end + 1, :] - cs)

    lane = lax.broadcasted_iota(jnp.int32, (q, LANES), 1)

    def pieces(v):
        hi, lo = _split_bf16(v, 2)
        return jnp.where(lane < SSD_HEADS, hi, lo)

    dt_p, ecs_p, dec_p = pieces(dt), pieces(ecs), pieces(dec)
    half = lane < SSD_HEAD_DIM
    gw = (SSD_HEADS // SSD_GROUPS) * SSD_HEAD_DIM
    o_b = SSD_INNER
    o_c = SSD_INNER + SSD_GROUPS * SSD_STATE

    for g in range(SSD_GROUPS):
        gs = slice(g * gw, (g + 1) * gw)
        e_g = e_ref[:, gs]
        dt_g = _dot(dt_p, e_g)
        ecs_g = _dot(ecs_p, e_g)
        dec_g = _dot(dec_p, e_g)
        xs_g = xact_ref[:, gs].astype(F32)
        xd_g = xs_g * dt_g
        xd_b = xd_g.astype(BF16)
        xdd_b = (xd_g * dec_g).astype(BF16)
        cg_b = xact_ref[:, o_c + g * SSD_STATE:o_c + (g + 1) * SSD_STATE]
        bgt_b = xact_ref[:, o_b + g * SSD_STATE:o_b + (g + 1) * SSD_STATE].astype(F32).T.astype(BF16)
        cbm = _dot(cg_b, bgt_b)
        h_in = h_ref[g]
        y_off = _dot(cg_b, h_in.astype(BF16))
        st = _dot(bgt_b, xdd_b)
        h_ref[g] = h_in * ecs_g[end:end + 1, :] + st
        for jj in range(gw // LANES):
            js = slice(jj * LANES, (jj + 1) * LANES)
            cidx = g * gw + jj * LANES
            xd_blk = xd_b[:, js]
            yd = None
            for hh in range(2):
                h = (cidx // SSD_HEAD_DIM) + hh
                diff = cs[:, h:h + 1] - cs_t[h:h + 1, :]
                m = (cbm * jnp.where(keep, jnp.exp2(diff), 0.0)).astype(BF16)
                rhs = jnp.where(half if hh == 0 else jnp.logical_not(half), xd_blk, jnp.zeros_like(xd_blk))
                t = _dot(m, rhs)
                yd = t if yd is None else yd + t
            y_blk = yd + y_off[:, js] * ecs_g[:, js] + dsk_ref[:, cidx:cidx + LANES] * xs_g[:, js]
            emit(cidx, y_blk)


def _ssd_bwd_kernel(xm_ref, xp_ref, xn_ref, dt_ref, cw_ref, cb_ref, a_ref, dtb_ref, dsk_ref, e_ref,
                    y_ref, xact_ref, ext_ref, h_ref):
    nc = pl.num_programs(1)
    _conv_silu(nc - 1 - pl.program_id(1), nc, xm_ref, xp_ref, xn_ref, cw_ref, cb_ref, ext_ref, xact_ref)

    def emit(cidx, y_blk):
        y_ref[:, cidx:cidx + LANES] = y_blk

    _ssd_chunk(True, xact_ref, dt_ref, a_ref, dtb_ref, dsk_ref, e_ref, h_ref, emit)


def _ssd_fwd_kernel(xact_ref, dt_ref, a_ref, dtb_ref, dsk_ref, e_ref, z_ref, yb_ref, ng_ref,
                    y_ref, h_ref, yz_ref):
    def emit(cidx, y_blk):
        z = z_ref[:, cidx:cidx + LANES].astype(F32)
        yz_ref[:, cidx:cidx + LANES] = (y_blk + yb_ref[:, cidx:cidx + LANES]) * _silu(z)

    _ssd_chunk(False, xact_ref, dt_ref, a_ref, dtb_ref, dsk_ref, e_ref, h_ref, emit)
    y_ref[...] = _rms_norm_rows(yz_ref[...], ng_ref[...]).astype(BF16)


def _ssd_backward_sweep(proj, dt_raw, conv_w, conv_b, a_row, dtb_row, dsk_row, e_mat, s):
    t = proj.shape[0]
    q = SSD_CHUNK
    nc = s // q
    hb = q // HALO
    nhb = s // HALO
    ch = lambda c: nc - 1 - c
    main = lambda bi, c: (bi * nc + ch(c), 0)
    prev = lambda bi, c: (bi * nhb + jnp.maximum(ch(c) * hb - 1, 0), 0)
    nxt = lambda bi, c: (bi * nhb + jnp.minimum((ch(c) + 1) * hb, nhb - 1), 0)
    const = lambda bi, c: (0, 0)
    return pl.pallas_call(
        _ssd_bwd_kernel,
        grid=(t // s, nc),
        in_specs=[
            pl.BlockSpec((q, SSD_XBC), main),
            pl.BlockSpec((HALO, SSD_XBC), prev),
            pl.BlockSpec((HALO, SSD_XBC), nxt),
            pl.BlockSpec((q, LANES), lambda bi, c: (bi * nc + ch(c), 1)),
            pl.BlockSpec((SSD_CONV, SSD_XBC), const),
            pl.BlockSpec((1, SSD_XBC), const),
            pl.BlockSpec((1, LANES), const),
            pl.BlockSpec((1, LANES), const),
            pl.BlockSpec((1, SSD_INNER), const),
            pl.BlockSpec((LANES, SSD_INNER), const),
        ],
        out_specs=[pl.BlockSpec((q, SSD_INNER), main), pl.BlockSpec((q, SSD_XBC), main)],
        out_shape=[jax.ShapeDtypeStruct((t, SSD_INNER), F32), jax.ShapeDtypeStruct((t, SSD_XBC), BF16)],
        scratch_shapes=[pltpu.VMEM((q + 2 * HALO, SSD_XBC), F32),
                        pltpu.VMEM((SSD_GROUPS, SSD_STATE, SSD_INNER // SSD_GROUPS), F32)],
        compiler_params=pltpu.CompilerParams(
            dimension_semantics=("parallel", "arbitrary"), vmem_limit_bytes=VMEM_LIMIT),
        name="ssd_backward_sweep",
    )(proj, proj, proj, dt_raw, conv_w, conv_b, a_row, dtb_row, dsk_row, e_mat)


def _ssd_forward_sweep(xact, dt_raw, proj, yb, a_row, dtb_row, dsk_row, e_mat, norm_g, s):
    t = xact.shape[0]
    q = SSD_CHUNK
    nc = s // q
    main = lambda bi, c: (bi * nc + c, 0)
    const = lambda bi, c: (0, 0)
    return pl.pallas_call(
        _ssd_fwd_kernel,
        grid=(t // s, nc),
        in_specs=[
            pl.BlockSpec((q, SSD_XBC), main),
            pl.BlockSpec((q, LANES), main),
            pl.BlockSpec((1, LANES), const),
            pl.BlockSpec((1, LANES), const),
            pl.BlockSpec((1, SSD_INNER), const),
            pl.BlockSpec((LANES, SSD_INNER), const),
            pl.BlockSpec((q, SSD_INNER), lambda bi, c: (bi * nc + c, 2)),
            pl.BlockSpec((q, SSD_INNER), main),
            pl.BlockSpec((1, SSD_INNER), const),
        ],
        out_specs=pl.BlockSpec((q, SSD_INNER), main),
        out_shape=jax.ShapeDtypeStruct((t, SSD_INNER), BF16),
        scratch_shapes=[pltpu.VMEM((SSD_GROUPS, SSD_STATE, SSD_INNER // SSD_GROUPS), F32),
                        pltpu.VMEM((q, SSD_INNER), F32)],
        compiler_params=pltpu.CompilerParams(
            dimension_semantics=("parallel", "arbitrary"), vmem_limit_bytes=VMEM_LIMIT),
        name="ssd_forward_sweep",
    )(xact, dt_raw, a_row, dtb_row, dsk_row, e_mat, proj, yb, norm_g)


def _even_out_kernel(ya_ref, bg_ref, cgm_ref, cgp_ref, cgn_ref, hm_ref, hp_ref, hn_ref, gt_ref, x_ref,
                     scw_ref, wa_ref, wb_ref, lg_ref, lb_ref, o_ref, ext_ref, *, tiles_per_seq):
    tm = ya_ref.shape[0]
    i = pl.program_id(0) % tiles_per_seq
    ext_ref[HALO:HALO + tm, :] = cgm_ref[...].astype(F32) * hm_ref[...].astype(F32)
    ext_ref[0:HALO, :] = jnp.where(i > 0, cgp_ref[...].astype(F32) * hp_ref[...].astype(F32), 0.0)
    ext_ref[HALO + tm:2 * HALO + tm, :] = jnp.where(
        i < tiles_per_seq - 1, cgn_ref[...].astype(F32) * hn_ref[...].astype(F32), 0.0)
    conv = scw_ref[0:1, :] * ext_ref[HALO - 1:HALO - 1 + tm, :]
    conv = conv + scw_ref[1:2, :] * ext_ref[HALO:HALO + tm, :]
    conv = conv + scw_ref[2:3, :] * ext_ref[HALO + 1:HALO + 1 + tm, :]
    y_b = bg_ref[...].astype(F32) * conv * _silu(gt_ref[...].astype(F32))
    h = _dot(ya_ref[...], wa_ref[...]) + _dot(y_b.astype(BF16), wb_ref[...])
    o_ref[...] = _layer_norm_rows(ALPHA * x_ref[...] + h, lg_ref[...], lb_ref[...])


def _even_out(ya, proj, x2d, sc_w, w_a, w_b, ln_g, ln_b, s, tm):
    t = x2d.shape[0]
    tps = s // tm
    hb = tm // HALO
    nh = t // HALO

    def colblk(j):
        return lambda i: (i, j)

    def prev(j):
        return lambda i: (jnp.maximum(i * hb - 1, 0), j)

    def nxt(j):
        return lambda i: (jnp.minimum((i + 1) * hb, nh - 1), j)

    w = SC_WIDTH
    const = lambda i: (0, 0)
    return pl.pallas_call(
        functools.partial(_even_out_kernel, tiles_per_seq=tps),
        grid=(t // tm,),
        in_specs=[
            pl.BlockSpec((tm, w), colblk(0)),
            pl.BlockSpec((tm, w), colblk(3)),
            pl.BlockSpec((tm, w), colblk(4)), pl.BlockSpec((HALO, w), prev(4)), pl.BlockSpec((HALO, w), nxt(4)),
            pl.BlockSpec((tm, w), colblk(5)), pl.BlockSpec((HALO, w), prev(5)), pl.BlockSpec((HALO, w), nxt(5)),
            pl.BlockSpec((tm, w), colblk(6)),
            pl.BlockSpec((tm, D_MODEL), colblk(0)),
            pl.BlockSpec((3, w), const),
            pl.BlockSpec((SSD_INNER, D_MODEL), const),
            pl.BlockSpec((w, D_MODEL), const),
            pl.BlockSpec((1, D_MODEL), const),
            pl.BlockSpec((1, D_MODEL), const),
        ],
        out_specs=pl.BlockSpec((tm, D_MODEL), colblk(0)),
        out_shape=jax.ShapeDtypeStruct((t, D_MODEL), F32),
        scratch_shapes=[pltpu.VMEM((tm + 2 * HALO, w), F32)],
        compiler_params=pltpu.CompilerParams(
            dimension_semantics=("parallel",), vmem_limit_bytes=VMEM_LIMIT),
        name="even_out_proj",
    )(ya, proj, proj, proj, proj, proj, proj, proj, proj, x2d, sc_w, w_a, w_b, ln_g, ln_b)


_O_CQ = 0
_O_CKV = MLA_Q_RANK
_O_KRA = _O_CKV + MLA_KV_RANK
_O_KRB = _O_KRA + LANES
_O_REST = _O_KRB + LANES
_ODD_REST = MLA_WIDTH + 2 * POOL_WIDTH
_ODD_COLS = _O_REST + _ODD_REST
_HEAD_W = LANES


def _odd_in_kernel(x_ref, pos_ref, w_ref, qg_ref, kg_ref, wq_ref, wqs_ref, wkn_ref, wv_ref, frq_ref,
                   sgn_ref, vone_ref, q_ref, k_ref, v_ref, r_ref):
    proj = _dot(x_ref[...].astype(BF16), w_ref[...])
    r_ref[...] = proj[:, _O_REST:].astype(BF16)
    ang = pos_ref[...].astype(F32) * frq_ref[...]
    cosf = jnp.cos(ang)
    sinf = jnp.sin(ang) * sgn_ref[...]
    cqn = _rms_norm_rows(proj[:, _O_CQ:_O_CQ + MLA_Q_RANK], qg_ref[...]).astype(BF16)
    qa = _dot(cqn, wq_ref[...])
    qb = _dot(cqn, wqs_ref[...])
    ckvn = _rms_norm_rows(proj[:, _O_CKV:_O_CKV + MLA_KV_RANK], kg_ref[...]).astype(BF16)
    kn = _dot(ckvn, wkn_ref[...])
    vv = _dot(ckvn, wv_ref[...]) + vone_ref[...]
    kr = proj[:, _O_KRA:_O_KRA + LANES] * cosf + proj[:, _O_KRB:_O_KRB + LANES] * sinf
    for h in range(MLA_HEADS):
        sl = slice(h * _HEAD_W, (h + 1) * _HEAD_W)
        q_ref[h] = ((qa[:, sl] * cosf + qb[:, sl] * sinf) * (ATTN_SCALE * LOG2E)).astype(BF16)
        k_ref[h] = (kn[:, sl] + kr).astype(BF16)
        v_ref[h] = vv[:, sl].astype(BF16)


def _odd_in(x2d, pos2d, w2, qg, kg, wq, wqs, wkn, wv, frq, sgn, vone, s, tm):
    t = x2d.shape[0]
    b = t // s
    tps = s // tm
    const = lambda i: (0, 0)
    hw = MLA_HEADS * _HEAD_W
    qkv_spec = pl.BlockSpec((None, MLA_HEADS, tm, _HEAD_W), lambda i: (i // tps, 0, i % tps, 0))
    qkv_shape = jax.ShapeDtypeStruct((b, MLA_HEADS, s, _HEAD_W), BF16)
    return pl.pallas_call(
        _odd_in_kernel,
        grid=(t // tm,),
        in_specs=[
            pl.BlockSpec((tm, D_MODEL), lambda i: (i, 0)),
            pl.BlockSpec((tm, 1), lambda i: (i, 0)),
            pl.BlockSpec((D_MODEL, _ODD_COLS), const),
            pl.BlockSpec((1, MLA_Q_RANK), const),
            pl.BlockSpec((1, MLA_KV_RANK), const),
            pl.BlockSpec((MLA_Q_RANK, hw), const),
            pl.BlockSpec((MLA_Q_RANK, hw), const),
            pl.BlockSpec((MLA_KV_RANK, hw), const),
            pl.BlockSpec((MLA_KV_RANK, hw), const),
            pl.BlockSpec((1, LANES), const),
            pl.BlockSpec((1, LANES), const),
            pl.BlockSpec((1, hw), const),
        ],
        out_specs=[qkv_spec, qkv_spec, qkv_spec, pl.BlockSpec((tm, _ODD_REST), lambda i: (i, 0))],
        out_shape=[qkv_shape, qkv_shape, qkv_shape, jax.ShapeDtypeStruct((t, _ODD_REST), BF16)],
        compiler_params=pltpu.CompilerParams(
            dimension_semantics=("parallel",), vmem_limit_bytes=VMEM_LIMIT),
        name="odd_in_proj_qkv",
    )(x2d, pos2d, w2, qg, kg, wq, wqs, wkn, wv, frq, sgn, vone)


def _attn_kernel(q_ref, k_ref, v_ref, o_ref, s_ref, p_ref, m_ref, a_ref, acc_ref, *, tq, tk, strip, unroll):
    seq = k_ref.shape[0]
    nq = seq // tq
    nk = seq // tk
    ipq = nk // unroll
    ncol = tk // LANES
    odd = (pl.program_id(1) % 2) == 1

    def scores(qi, kt, slot):
        qo = pl.multiple_of(qi * tq, tq)
        ko = pl.multiple_of(kt * tk, tk)
        s_ref[slot] = lax.dot_general(q_ref[pl.ds(qo, tq), :], k_ref[pl.ds(ko, tk), :],
                                      (((1,), (1,)), ((), ())), preferred_element_type=F32)

    def accumulate(kt, slot):
        vo = pl.multiple_of(kt * tk, tk)
        acc_ref[...] = acc_ref[...] * a_ref[...] + _dot(p_ref[slot], v_ref[pl.ds(vo, tk), :])

    def emit(qi):
        acc = acc_ref[...]
        denom = jnp.where(odd, acc[:, 0:1], acc[:, MLA_V:MLA_V + 1])
        o_ref[pl.ds(pl.multiple_of(qi * tq, tq), tq), :] = (acc / denom).astype(BF16)

    def softmax(slot, restart):
        for r in range(tq // strip):
            rows = slice(r * strip, (r + 1) * strip)
            blks = [s_ref[slot, rows, c * LANES:(c + 1) * LANES] for c in range(ncol)]
            mx = blks[0]
            for c in range(1, ncol):
                mx = jnp.maximum(mx, blks[c])
            m_old = m_ref[rows, :]
            if restart is not None:
                m_old = jnp.where(restart, -jnp.inf, m_old)
            m_new = jnp.maximum(m_old, jnp.max(mx, axis=1, keepdims=True))
            m_ref[rows, :] = m_new
            a_ref[rows, :] = jnp.exp2(m_old - m_new)
            for c in range(ncol):
                p_ref[slot, rows, c * LANES:(c + 1) * LANES] = jnp.exp2(blks[c] - m_new).astype(BF16)

    m_ref[...] = jnp.zeros(m_ref.shape, F32)
    a_ref[...] = jnp.ones(a_ref.shape, F32)
    acc_ref[...] = jnp.ones(acc_ref.shape, F32)
    p_ref[1] = jnp.zeros(p_ref.shape[1:], BF16)
    scores(0, 0, 0)

    def body(j, carry):
        qi = j // ipq
        jj = j % ipq
        first = jj == 0
        for u in range(unroll):
            slot = u % 2
            t = jj * unroll + u
            if u == 0:
                accumulate(jnp.where(first, nk - 1, t - 1), 1 - slot)
                emit(jnp.where(first, jnp.maximum(qi - 1, 0), qi))
            else:
                accumulate(t - 1, 1 - slot)
            if u == unroll - 1:
                last = jj == ipq - 1
                scores(jnp.where(last, jnp.minimum(qi + 1, nq - 1), qi), jnp.where(last, 0, t + 1), 1 - slot)
            else:
                scores(qi, t + 1, 1 - slot)
            softmax(slot, first if u == 0 else None)
        return carry

    lax.fori_loop(0, nq * ipq, body, 0)
    accumulate(nk - 1, 1 - unroll % 2)
    emit(nq - 1)


def _attention(q, k, v, tq, tk, strip, unroll):
    b, h, s, w = q.shape
    assert unroll % 2 == 0 and (s // tk) % unroll == 0 and s % tq == 0 and tq % strip == 0
    spec = pl.BlockSpec((None, None, s, w), lambda bi, hi: (bi, hi, 0, 0))
    return pl.pallas_call(
        functools.partial(_attn_kernel, tq=tq, tk=tk, strip=strip, unroll=unroll),
        grid=(b, h),
        in_specs=[spec, spec, spec],
        out_specs=spec,
        out_shape=jax.ShapeDtypeStruct((b, h, s, w), BF16),
        scratch_shapes=[
            pltpu.VMEM((2, tq, tk), F32),
            pltpu.VMEM((2, tq, tk), BF16),
            pltpu.VMEM((tq, LANES), F32),
            pltpu.VMEM((tq, LANES), F32),
            pltpu.VMEM((tq, w), F32),
        ],
        compiler_params=pltpu.CompilerParams(
            dimension_semantics=("parallel", "parallel"), vmem_limit_bytes=VMEM_LIMIT),
        name="mla_attention",
    )(q, k, v)


def _odd_out_kernel(o_ref, gc_ref, um_ref, up_ref, un_ref, gd_ref, x_ref, pw_ref, ps_ref, wc_ref, wd_ref,
                    lg_ref, lb_ref, out_ref, ext_ref, *, tiles_per_seq, seq):
    tm = x_ref.shape[0]
    i = pl.program_id(0) % tiles_per_seq
    lane = lax.broadcasted_iota(jnp.int32, (tm, LANES), 1)
    low = lane < MLA_V
    gc = _silu(gc_ref[...].astype(F32))
    acc = None
    for j in range(MLA_HEADS // 2):
        pair = jnp.where(low, o_ref[2 * j], o_ref[2 * j + 1]).astype(F32)
        yc = (pair * gc[:, j * LANES:(j + 1) * LANES]).astype(BF16)
        t = _dot(yc, wc_ref[j * LANES:(j + 1) * LANES, :])
        acc = t if acc is None else acc + t

    um = um_ref[...].astype(F32)
    ext_ref[HALO:HALO + tm, :] = um
    ext_ref[0:HALO, :] = jnp.where(i > 0, up_ref[...].astype(F32), 0.0)
    ext_ref[HALO + tm:2 * HALO + tm, :] = jnp.where(i < tiles_per_seq - 1, un_ref[...].astype(F32), 0.0)
    pos = i * tm + lax.broadcasted_iota(jnp.int32, (tm, LANES), 0)
    gd = _silu(gd_ref[...].astype(F32))
    for gi, w in enumerate(POOL_WINDOWS):
        cs = slice(gi * POOL_GROUP, (gi + 1) * POOL_GROUP)
        wsum = None
        for d in range(-(w // 2), w - w // 2):
            term = ext_ref[HALO + d:HALO + d + tm, cs]
            wsum = term if wsum is None else wsum + term
        cnt = (jnp.minimum(pos + (w - w // 2), seq) - jnp.maximum(pos - w // 2, 0)).astype(F32)
        pooled = wsum / cnt - um[:, cs]
        yd = _dot(pooled.astype(BF16), pw_ref[gi]) * ps_ref[:, cs] * gd[:, cs]
        acc = acc + _dot(yd.astype(BF16), wd_ref[cs, :])
    out_ref[...] = _layer_norm_rows(ALPHA * x_ref[...] + acc, lg_ref[...], lb_ref[...])


def _odd_out(o, rest, x2d, pool_w, pool_scale, w_c, w_d, ln_g, ln_b, s, tm):
    t = x2d.shape[0]
    tps = s // tm
    hb = tm // HALO
    nh = t // HALO
    const = lambda i: (0, 0)
    pw = POOL_WIDTH
    return pl.pallas_call(
        functools.partial(_odd_out_kernel, tiles_per_seq=tps, seq=s),
        grid=(t // tm,),
        in_specs=[
            pl.BlockSpec((None, MLA_HEADS, tm, _HEAD_W), lambda i: (i // tps, 0, i % tps, 0)),
            pl.BlockSpec((tm, pw), lambda i: (i, 0)),
            pl.BlockSpec((tm, pw), lambda i: (i, 1)),
            pl.BlockSpec((HALO, pw), lambda i: (jnp.maximum(i * hb - 1, 0), 1)),
            pl.BlockSpec((HALO, pw), lambda i: (jnp.minimum((i + 1) * hb, nh - 1), 1)),
            pl.BlockSpec((tm, pw), lambda i: (i, 2)),
            pl.BlockSpec((tm, D_MODEL), lambda i: (i, 0)),
            pl.BlockSpec((len(POOL_WINDOWS), POOL_GROUP, POOL_GROUP), lambda i: (0, 0, 0)),
            pl.BlockSpec((1, pw), const),
            pl.BlockSpec((MLA_WIDTH, D_MODEL), const),
            pl.BlockSpec((pw, D_MODEL), const),
            pl.BlockSpec((1, D_MODEL), const),
            pl.BlockSpec((1, D_MODEL), const),
        ],
        out_specs=pl.BlockSpec((tm, D_MODEL), lambda i: (i, 0)),
        out_shape=jax.ShapeDtypeStruct((t, D_MODEL), F32),
        scratch_shapes=[pltpu.VMEM((tm + 2 * HALO, pw), F32)],
        compiler_params=pltpu.CompilerParams(
            dimension_semantics=("parallel",), vmem_limit_bytes=VMEM_LIMIT),
        name="odd_out_proj",
    )(o, rest, rest, rest, rest, rest, x2d, pool_w, pool_scale, w_c, w_d, ln_g, ln_b)


def _rep3(v):
    z = jnp.zeros((LANES - 3 * SSD_HEADS,), F32)
    return jnp.concatenate([v, v, v, z])[None, :]


def _even_params(w_in, conv_w, conv_b, a_log, dt_bias, d_skip, norm_g, sc_conv_w, w_out):
    o_z = 0
    o_xbc = SSD_INNER
    o_dt = o_xbc + SSD_XBC
    o_sc = o_dt + 2 * SSD_HEADS
    w_main = jnp.concatenate([w_in[:, o_xbc:o_dt], w_in[:, o_z:o_xbc], w_in[:, o_sc:]], axis=1).astype(BF16)
    zpad = jnp.zeros((D_MODEL, LANES - 3 * SSD_HEADS), F32)
    wdt = []
    for k in range(2):
        wk = w_in[:, o_dt + k * SSD_HEADS:o_dt + (k + 1) * SSD_HEADS]
        wdt.append(jnp.concatenate([wk, wk, wk, zpad], axis=1))
    wdt = jnp.concatenate(wdt, axis=1)
    wdt_hi = wdt.astype(BF16)
    wdt_lo = (wdt - wdt_hi.astype(F32)).astype(BF16)
    a_rows = [_rep3(-jnp.exp(a_log[k].astype(F32)) * LOG2E) for k in range(2)]
    dtb_rows = [_rep3(dt_bias[k].astype(F32)) for k in range(2)]
    dsk_rows = [jnp.repeat(d_skip[k].astype(F32), SSD_HEAD_DIM)[None, :] for k in range(2)]
    src = jnp.arange(LANES)[:, None]
    dst = jnp.arange(SSD_INNER)[None, :] // SSD_HEAD_DIM
    e_mat = jnp.where((src < 2 * SSD_HEADS) & (src % SSD_HEADS == dst), 1.0, 0.0).astype(BF16)
    return dict(
        w_main=w_main, wdt_hi=wdt_hi, wdt_lo=wdt_lo, conv_w=conv_w.astype(F32), conv_b=conv_b[None, :].astype(F32),
        a_rows=a_rows, dtb_rows=dtb_rows, dsk_rows=dsk_rows, e_mat=e_mat, norm_g=norm_g[None, :].astype(F32),
        sc_w=sc_conv_w.astype(F32), w_a=w_out[:SSD_INNER].astype(BF16), w_b=w_out[SSD_INNER:].astype(BF16))


def _odd_params(w_in, q_norm_g, w_uq, kv_norm_g, w_ukv, pool_w, pool_scale, w_out):
    o_ckv = MLA_Q_RANK
    o_kr = o_ckv + MLA_KV_RANK
    o_rest = o_kr + MLA_ROPE
    half = MLA_ROPE // 2
    z64 = jnp.zeros((D_MODEL, MLA_NOPE), F32)
    z32 = jnp.zeros((D_MODEL, LANES - MLA_NOPE - MLA_ROPE), F32)
    kr1 = w_in[:, o_kr:o_kr + half]
    kr2 = w_in[:, o_kr + half:o_kr + MLA_ROPE]
    w2 = jnp.concatenate([w_in[:, :o_kr], z64, kr1, kr2, z32, z64, kr2, kr1, z32, w_in[:, o_rest:]],
                         axis=1).astype(BF16)
    qd = MLA_NOPE + MLA_ROPE
    w3 = w_uq.reshape(MLA_Q_RANK, MLA_HEADS, qd)
    zq = lambda n: jnp.zeros((MLA_Q_RANK, MLA_HEADS, n), F32)
    wq = jnp.concatenate([w3, zq(_HEAD_W - qd)], axis=-1).reshape(MLA_Q_RANK, -1).astype(BF16)
    wqs = jnp.concatenate([zq(MLA_NOPE), w3[..., MLA_NOPE + half:], w3[..., MLA_NOPE:MLA_NOPE + half],
                           zq(_HEAD_W - qd)], axis=-1).reshape(MLA_Q_RANK, -1).astype(BF16)
    w4 = w_ukv.reshape(MLA_KV_RANK, MLA_HEADS, MLA_NOPE + MLA_V)
    zk = jnp.zeros((MLA_KV_RANK, MLA_HEADS, _HEAD_W - MLA_NOPE), F32)
    wkn = jnp.concatenate([w4[..., :MLA_NOPE], zk], axis=-1).reshape(MLA_KV_RANK, -1).astype(BF16)
    zv = jnp.zeros((MLA_KV_RANK, _HEAD_W - MLA_V), F32)
    wv_blocks = []
    vone = []
    for h in range(MLA_HEADS):
        vh = w4[:, h, MLA_NOPE:]
        wv_blocks.append(jnp.concatenate([vh, zv] if h % 2 == 0 else [zv, vh], axis=-1))
        vone.append(jnp.zeros((_HEAD_W,), F32).at[MLA_V if h % 2 == 0 else 0].set(1.0))
    wv = jnp.concatenate(wv_blocks, axis=-1).astype(BF16)
    vone = jnp.concatenate(vone)[None, :]
    inv_freq = ROPE_THETA ** (-jnp.arange(half, dtype=F32) / half)
    zl = lambda n: jnp.zeros((n,), F32)
    frq = jnp.concatenate([zl(MLA_NOPE), inv_freq, inv_freq, zl(LANES - qd)])[None, :]
    sgn = jnp.concatenate([zl(MLA_NOPE), -jnp.ones((half,), F32), jnp.ones((half,), F32), zl(LANES - qd)])[None, :]
    return dict(
        w2=w2, qg=q_norm_g[None, :].astype(F32), kg=kv_norm_g[None, :].astype(F32), wq=wq, wqs=wqs, wkn=wkn,
        wv=wv, vone=vone, frq=frq, sgn=sgn, pool_w=pool_w.astype(BF16), pool_scale=pool_scale[None, :].astype(F32),
        w_c=w_out[:MLA_WIDTH].astype(BF16), w_d=w_out[MLA_WIDTH:].astype(BF16))


def _pick(n, pref):
    t = min(pref, n)
    while n % t:
        t //= 2
    return t


def _even_layer(x2d, s, p, ln_g, ln_b):
    t = x2d.shape[0]
    proj, dt_raw = _even_in_proj(x2d, p["w_main"], p["wdt_hi"], p["wdt_lo"], _pick(t, 1024), 1024)
    yb, xact = _ssd_backward_sweep(proj, dt_raw, p["conv_w"], p["conv_b"], p["a_rows"][1], p["dtb_rows"][1],
                                   p["dsk_rows"][1], p["e_mat"], s)
    ya = _ssd_forward_sweep(xact, dt_raw, proj, yb, p["a_rows"][0], p["dtb_rows"][0], p["dsk_rows"][0],
                            p["e_mat"], p["norm_g"], s)
    return _even_out(ya, proj, x2d, p["sc_w"], p["w_a"], p["w_b"], ln_g[None, :], ln_b[None, :], s, _pick(s, 512))


def _odd_layer(x2d, pos2d, s, p, ln_g, ln_b):
    tm = _pick(s, 512)
    q, k, v, rest = _odd_in(x2d, pos2d, p["w2"], p["qg"], p["kg"], p["wq"], p["wqs"], p["wkn"], p["wv"],
                            p["frq"], p["sgn"], p["vone"], s, tm)
    o = _attention(q, k, v, _pick(s, 512), _pick(s // 8, 512), 64, 8)
    return _odd_out(o, rest, x2d, p["pool_w"], p["pool_scale"], p["w_c"], p["w_d"], ln_g[None, :], ln_b[None, :],
                    s, tm)


def kernel(x, positions, ev_w_in, ev_conv_w, ev_conv_b, ev_a_log, ev_dt_bias, ev_d_skip, ev_norm_g, ev_sc_conv_w,
           ev_w_out, ev_ln_g, ev_ln_b, od_w_in, od_q_norm_g, od_w_uq, od_kv_norm_g, od_w_ukv, od_pool_w,
           od_pool_scale, od_w_out, od_ln_g, od_ln_b):
    b, s, d = x.shape
    x2d = x.reshape(b * s, d)
    pos2d = positions.reshape(b * s, 1)
    for layer in range(DEPTH):
        i = layer // 2
        if layer % 2 == 0:
            p = _even_params(ev_w_in[i], ev_conv_w[i], ev_conv_b[i], ev_a_log[i], ev_dt_bias[i], ev_d_skip[i],
                             ev_norm_g[i], ev_sc_conv_w[i], ev_w_out[i])
            x2d = _even_layer(x2d, s, p, ev_ln_g[i], ev_ln_b[i])
        else:
            p = _odd_params(od_w_in[i], od_q_norm_g[i], od_w_uq[i], od_kv_norm_g[i], od_w_ukv[i], od_pool_w[i],
                            od_pool_scale[i], od_w_out[i])
            x2d = _odd_layer(x2d, pos2d, s, p, od_ln_g[i], od_ln_b[i])
    return x2d.reshape(b, s, d)
```

```python
import functools
import math

import jax
import jax.numpy as jnp
from jax import lax
from jax.experimental import pallas as pl
from jax.experimental.pallas import tpu as pltpu

F32 = jnp.float32
BF16 = jnp.bfloat16

D_MODEL = 1024
DEPTH = 2
SSD_HEADS = 16
SSD_HEAD_DIM = 64
SSD_INNER = SSD_HEADS * SSD_HEAD_DIM
SSD_GROUPS = 4
SSD_STATE = 128
SSD_CONV = 4
SSD_CONV_LEFT = 2
SSD_CHUNK = 128
SSD_STEP = 4
SSD_XBC = SSD_INNER + 2 * SSD_GROUPS * SSD_STATE
SC_WIDTH = 1024
MLA_HEADS = 8
MLA_Q_RANK = 256
MLA_KV_RANK = 128
MLA_NOPE = 64
MLA_ROPE = 32
MLA_V = 64
MLA_WIDTH = MLA_HEADS * MLA_V
ROPE_THETA = 10000.0
ATTN_SCALE = (MLA_NOPE + MLA_ROPE) ** -0.5
LOG2E = math.log2(math.e)
POOL_WINDOWS = (2, 4, 8, 16)
POOL_GROUP = 128
POOL_WIDTH = POOL_GROUP * len(POOL_WINDOWS)
EPS = 1e-5
ALPHA = (2 * DEPTH) ** 0.25

LANES = 128
HALO = 8
HALO_B = 16
VMEM_LIMIT = 56 * 1024 * 1024


def _dot(a, b):
    return jnp.dot(a, b, preferred_element_type=F32)


def _silu(v):
    h = 0.5 * v
    return h + h * jnp.tanh(h)


def _softplus(v):
    return jnp.maximum(v, 0.0) + jnp.log1p(jnp.exp(-jnp.abs(v)))


def _split_bf16(v, pieces):
    out = []
    r = v
    for _ in range(pieces):
        p = r.astype(BF16)
        out.append(p)
        r = r - p.astype(F32)
    return out


def _layer_norm_rows(r, g, b):
    mu = jnp.mean(r, axis=-1, keepdims=True)
    d = r - mu
    var = jnp.mean(d * d, axis=-1, keepdims=True)
    return d * lax.rsqrt(var + EPS) * g + b


def _rms_norm_rows(v, g):
    return v * lax.rsqrt(jnp.mean(v * v, axis=-1, keepdims=True) + EPS) * g


def _even_in_proj_kernel(x_ref, w_ref, wdh_ref, wdl_ref, o_ref, dt_ref, xb_ref):
    @pl.when(pl.program_id(1) == 0)
    def _():
        x = x_ref[...]
        xh = x.astype(BF16)
        xb_ref[...] = xh
        xl = (x - xh.astype(F32)).astype(BF16)
        wh = wdh_ref[...]
        dt_ref[...] = _dot(xh, wh) + _dot(xl, wh) + _dot(xh, wdl_ref[...])

    o_ref[...] = _dot(xb_ref[...], w_ref[...]).astype(BF16)


def _even_in_proj(x2d, w_main, wdt_hi, wdt_lo, tm, tn):
    t, k = x2d.shape
    n = w_main.shape[1]
    ndt = wdt_hi.shape[1]
    return pl.pallas_call(
        _even_in_proj_kernel,
        grid=(t // tm, n // tn),
        in_specs=[
            pl.BlockSpec((tm, k), lambda i, j: (i, 0)),
            pl.BlockSpec((k, tn), lambda i, j: (0, j)),
            pl.BlockSpec((k, ndt), lambda i, j: (0, 0)),
            pl.BlockSpec((k, ndt), lambda i, j: (0, 0)),
        ],
        out_specs=[
            pl.BlockSpec((tm, tn), lambda i, j: (i, j)),
            pl.BlockSpec((tm, ndt), lambda i, j: (i, 0)),
        ],
        out_shape=[jax.ShapeDtypeStruct((t, n), BF16), jax.ShapeDtypeStruct((t, ndt), F32)],
        scratch_shapes=[pltpu.VMEM((tm, k), BF16)],
        compiler_params=pltpu.CompilerParams(
            dimension_semantics=("parallel", "arbitrary"), vmem_limit_bytes=VMEM_LIMIT),
        name="even_in_proj",
    )(x2d, w_main, wdt_hi, wdt_lo)


def _conv_silu(first, last, xm_ref, xp_ref, xn_ref, sh_ref, cw_ref, cb_ref, xact_ref):
    rows = xm_ref.shape[0]
    q = SSD_CHUNK
    blk = 2 * LANES
    taps = [k for k in range(SSD_CONV) if k != SSD_CONV_LEFT]
    for j in range(SSD_XBC // blk):
        sl = slice(j * blk, (j + 1) * blk)
        xp = xp_ref[:, sl]
        xn = xn_ref[:, sl]
        ext = jnp.concatenate([jnp.where(first, jnp.zeros_like(xp), xp), xm_ref[:, sl],
                               jnp.where(last, jnp.zeros_like(xn), xn)], axis=0)
        for r0 in range(0, rows, q):
            win = ext[r0:r0 + q + 2 * HALO_B]
            acc = cb_ref[:, sl] + cw_ref[SSD_CONV_LEFT:SSD_CONV_LEFT + 1, sl] * xm_ref[r0:r0 + q, sl].astype(F32)
            for i, k in enumerate(taps):
                acc = acc + cw_ref[k:k + 1, sl] * _dot(sh_ref[i], win)
            xact_ref[r0:r0 + q, sl] = _silu(acc).astype(BF16)


def _ssd_chunk(reverse, r0, xact_ref, dt_ref, a_ref, dtb_ref, dsk_ref, e_ref, h_ref, emit):
    q = SSD_CHUNK
    rs = slice(r0, r0 + q)

    dt = _softplus(dt_ref[rs, :] + dtb_ref[...])
    da = dt * a_ref[...]
    row = lax.broadcasted_iota(jnp.int32, (q, q), 0)
    col = lax.broadcasted_iota(jnp.int32, (q, q), 1)
    keep = (col >= row) if reverse else (col <= row)
    tri = jnp.where(keep, 1.0, 0.0).astype(BF16)
    p1, p2, p3 = _split_bf16(da, 3)
    cs = _dot(tri, p1) + _dot(tri, p2) + _dot(tri, p3)
    cs_t = cs.T
    end = 0 if reverse else q - 1
    ecs = jnp.exp2(cs)
    dec = jnp.exp2(cs[end:end + 1, :] - cs)

    lane = lax.broadcasted_iota(jnp.int32, (q, LANES), 1)

    def pieces(v):
        hi, lo = _split_bf16(v, 2)
        return jnp.where(lane < SSD_HEADS, hi, lo)

    dt_p, ecs_p, dec_p = pieces(dt), pieces(ecs), pieces(dec)
    half = lane < SSD_HEAD_DIM
    gw = (SSD_HEADS // SSD_GROUPS) * SSD_HEAD_DIM
    o_b = SSD_INNER
    o_c = SSD_INNER + SSD_GROUPS * SSD_STATE

    for g in range(SSD_GROUPS):
        gs = slice(g * gw, (g + 1) * gw)
        e_g = e_ref[:, gs]
        dt_g = _dot(dt_p, e_g)
        ecs_g = _dot(ecs_p, e_g)
        dec_g = _dot(dec_p, e_g)
        xs_g = xact_ref[rs, gs].astype(F32)
        xd_g = xs_g * dt_g
        xd_b = xd_g.astype(BF16)
        xdd_b = (xd_g * dec_g).astype(BF16)
        cg_b = xact_ref[rs, o_c + g * SSD_STATE:o_c + (g + 1) * SSD_STATE]
        bgt_b = xact_ref[rs, o_b + g * SSD_STATE:o_b + (g + 1) * SSD_STATE].astype(F32).T.astype(BF16)
        cbm = _dot(cg_b, bgt_b)
        h_in = h_ref[g]
        y_off = _dot(cg_b, h_in.astype(BF16))
        st = _dot(bgt_b, xdd_b)
        h_ref[g] = h_in * ecs_g[end:end + 1, :] + st
        for jj in range(gw // LANES):
            js = slice(jj * LANES, (jj + 1) * LANES)
            cidx = g * gw + jj * LANES
            xd_blk = xd_b[:, js]
            yd = None
            for hh in range(2):
                h = (cidx // SSD_HEAD_DIM) + hh
                diff = cs[:, h:h + 1] - cs_t[h:h + 1, :]
                m = (cbm * jnp.where(keep, jnp.exp2(diff), 0.0)).astype(BF16)
                rhs = jnp.where(half if hh == 0 else jnp.logical_not(half), xd_blk, jnp.zeros_like(xd_blk))
                t = _dot(m, rhs)
                yd = t if yd is None else yd + t
            y_blk = yd + y_off[:, js] * ecs_g[:, js] + dsk_ref[:, cidx:cidx + LANES] * xs_g[:, js]
            emit(r0, cidx, y_blk)


def _reset_state(h_ref):
    @pl.when(pl.program_id(1) == 0)
    def _():
        h_ref[...] = jnp.zeros_like(h_ref)


def _ssd_bwd_kernel(xm_ref, xp_ref, xn_ref, sh_ref, dt_ref, cw_ref, cb_ref, a_ref, dtb_ref, dsk_ref, e_ref,
                    y_ref, xact_ref, h_ref):
    nt = pl.num_programs(1)
    tile = nt - 1 - pl.program_id(1)
    _conv_silu(tile == 0, tile == nt - 1, xm_ref, xp_ref, xn_ref, sh_ref, cw_ref, cb_ref, xact_ref)
    _reset_state(h_ref)

    def emit(r0, cidx, y_blk):
        y_ref[r0:r0 + SSD_CHUNK, cidx:cidx + LANES] = y_blk

    for sub in reversed(range(SSD_STEP)):
        _ssd_chunk(True, sub * SSD_CHUNK, xact_ref, dt_ref, a_ref, dtb_ref, dsk_ref, e_ref, h_ref, emit)


def _ssd_fwd_kernel(xact_ref, dt_ref, a_ref, dtb_ref, dsk_ref, e_ref, z_ref, yb_ref, ng_ref,
                    y_ref, h_ref, yz_ref):
    _reset_state(h_ref)

    def emit(r0, cidx, y_blk):
        rs = slice(r0, r0 + SSD_CHUNK)
        z = z_ref[rs, cidx:cidx + LANES].astype(F32)
        yz_ref[rs, cidx:cidx + LANES] = (y_blk + yb_ref[rs, cidx:cidx + LANES]) * _silu(z)

    for sub in range(SSD_STEP):
        _ssd_chunk(False, sub * SSD_CHUNK, xact_ref, dt_ref, a_ref, dtb_ref, dsk_ref, e_ref, h_ref, emit)
    y_ref[...] = _rms_norm_rows(yz_ref[...], ng_ref[...]).astype(BF16)


def _ssd_backward_sweep(proj, dt_raw, shifts, conv_w, conv_b, a_row, dtb_row, dsk_row, e_mat, s):
    t = proj.shape[0]
    rows = SSD_STEP * SSD_CHUNK
    nt = s // rows
    hb = rows // HALO_B
    nhb = s // HALO_B
    tl = lambda c: nt - 1 - c
    main = lambda bi, c: (bi * nt + tl(c), 0)
    prev = lambda bi, c: (bi * nhb + jnp.maximum(tl(c) * hb - 1, 0), 0)
    nxt = lambda bi, c: (bi * nhb + jnp.minimum((tl(c) + 1) * hb, nhb - 1), 0)
    const = lambda bi, c: (0, 0)
    return pl.pallas_call(
        _ssd_bwd_kernel,
        grid=(t // s, nt),
        in_specs=[
            pl.BlockSpec((rows, SSD_XBC), main),
            pl.BlockSpec((HALO_B, SSD_XBC), prev),
            pl.BlockSpec((HALO_B, SSD_XBC), nxt),
            pl.BlockSpec(shifts.shape, lambda bi, c: (0, 0, 0)),
            pl.BlockSpec((rows, LANES), lambda bi, c: (bi * nt + tl(c), 1)),
            pl.BlockSpec((SSD_CONV, SSD_XBC), const),
            pl.BlockSpec((1, SSD_XBC), const),
            pl.BlockSpec((1, LANES), const),
            pl.BlockSpec((1, LANES), const),
            pl.BlockSpec((1, SSD_INNER), const),
            pl.BlockSpec((LANES, SSD_INNER), const),
        ],
        out_specs=[pl.BlockSpec((rows, SSD_INNER), main), pl.BlockSpec((rows, SSD_XBC), main)],
        out_shape=[jax.ShapeDtypeStruct((t, SSD_INNER), F32), jax.ShapeDtypeStruct((t, SSD_XBC), BF16)],
        scratch_shapes=[pltpu.VMEM((SSD_GROUPS, SSD_STATE, SSD_INNER // SSD_GROUPS), F32)],
        compiler_params=pltpu.CompilerParams(
            dimension_semantics=("parallel", "arbitrary"), vmem_limit_bytes=VMEM_LIMIT),
        name="ssd_backward_sweep",
    )(proj, proj, proj, shifts, dt_raw, conv_w, conv_b, a_row, dtb_row, dsk_row, e_mat)


def _ssd_forward_sweep(xact, dt_raw, proj, yb, a_row, dtb_row, dsk_row, e_mat, norm_g, s):
    t = xact.shape[0]
    rows = SSD_STEP * SSD_CHUNK
    nt = s // rows
    main = lambda bi, c: (bi * nt + c, 0)
    const = lambda bi, c: (0, 0)
    return pl.pallas_call(
        _ssd_fwd_kernel,
        grid=(t // s, nt),
        in_specs=[
            pl.BlockSpec((rows, SSD_XBC), main),
            pl.BlockSpec((rows, LANES), main),
            pl.BlockSpec((1, LANES), const),
            pl.BlockSpec((1, LANES), const),
            pl.BlockSpec((1, SSD_INNER), const),
            pl.BlockSpec((LANES, SSD_INNER), const),
            pl.BlockSpec((rows, SSD_INNER), lambda bi, c: (bi * nt + c, 2)),
            pl.BlockSpec((rows, SSD_INNER), main),
            pl.BlockSpec((1, SSD_INNER), const),
        ],
        out_specs=pl.BlockSpec((rows, SSD_INNER), main),
        out_shape=jax.ShapeDtypeStruct((t, SSD_INNER), BF16),
        scratch_shapes=[pltpu.VMEM((SSD_GROUPS, SSD_STATE, SSD_INNER // SSD_GROUPS), F32),
                        pltpu.VMEM((rows, SSD_INNER), F32)],
        compiler_params=pltpu.CompilerParams(
            dimension_semantics=("parallel", "arbitrary"), vmem_limit_bytes=VMEM_LIMIT),
        name="ssd_forward_sweep",
    )(xact, dt_raw, a_row, dtb_row, dsk_row, e_mat, proj, yb, norm_g)


def _even_out_kernel(ya_ref, bg_ref, cgm_ref, cgp_ref, cgn_ref, hm_ref, hp_ref, hn_ref, gt_ref, x_ref,
                     scw_ref, wa_ref, wb_ref, lg_ref, lb_ref, o_ref, ext_ref, *, tiles_per_seq):
    tm = ya_ref.shape[0]
    i = pl.program_id(0) % tiles_per_seq
    ext_ref[HALO:HALO + tm, :] = cgm_ref[...].astype(F32) * hm_ref[...].astype(F32)
    ext_ref[0:HALO, :] = jnp.where(i > 0, cgp_ref[...].astype(F32) * hp_ref[...].astype(F32), 0.0)
    ext_ref[HALO + tm:2 * HALO + tm, :] = jnp.where(
        i < tiles_per_seq - 1, cgn_ref[...].astype(F32) * hn_ref[...].astype(F32), 0.0)
    conv = scw_ref[0:1, :] * ext_ref[HALO - 1:HALO - 1 + tm, :]
    conv = conv + scw_ref[1:2, :] * ext_ref[HALO:HALO + tm, :]
    conv = conv + scw_ref[2:3, :] * ext_ref[HALO + 1:HALO + 1 + tm, :]
    y_b = bg_ref[...].astype(F32) * conv * _silu(gt_ref[...].astype(F32))
    h = _dot(ya_ref[...], wa_ref[...]) + _dot(y_b.astype(BF16), wb_ref[...])
    o_ref[...] = _layer_norm_rows(ALPHA * x_ref[...] + h, lg_ref[...], lb_ref[...])


def _even_out(ya, proj, x2d, sc_w, w_a, w_b, ln_g, ln_b, s, tm):
    t = x2d.shape[0]
    tps = s // tm
    hb = tm // HALO
    nh = t // HALO

    def colblk(j):
        return lambda i: (i, j)

    def prev(j):
        return lambda i: (jnp.maximum(i * hb - 1, 0), j)

    def nxt(j):
        return lambda i: (jnp.minimum((i + 1) * hb, nh - 1), j)

    w = SC_WIDTH
    const = lambda i: (0, 0)
    return pl.pallas_call(
        functools.partial(_even_out_kernel, tiles_per_seq=tps),
        grid=(t // tm,),
        in_specs=[
            pl.BlockSpec((tm, w), colblk(0)),
            pl.BlockSpec((tm, w), colblk(3)),
            pl.BlockSpec((tm, w), colblk(4)), pl.BlockSpec((HALO, w), prev(4)), pl.BlockSpec((HALO, w), nxt(4)),
            pl.BlockSpec((tm, w), colblk(5)), pl.BlockSpec((HALO, w), prev(5)), pl.BlockSpec((HALO, w), nxt(5)),
            pl.BlockSpec((tm, w), colblk(6)),
            pl.BlockSpec((tm, D_MODEL), colblk(0)),
            pl.BlockSpec((3, w), const),
            pl.BlockSpec((SSD_INNER, D_MODEL), const),
            pl.BlockSpec((w, D_MODEL), const),
            pl.BlockSpec((1, D_MODEL), const),
            pl.BlockSpec((1, D_MODEL), const),
        ],
        out_specs=pl.BlockSpec((tm, D_MODEL), colblk(0)),
        out_shape=jax.ShapeDtypeStruct((t, D_MODEL), F32),
        scratch_shapes=[pltpu.VMEM((tm + 2 * HALO, w), F32)],
        compiler_params=pltpu.CompilerParams(
            dimension_semantics=("parallel",), vmem_limit_bytes=VMEM_LIMIT),
        name="even_out_proj",
    )(ya, proj, proj, proj, proj, proj, proj, proj, proj, x2d, sc_w, w_a, w_b, ln_g, ln_b)


_O_CQ = 0
_O_CKV = MLA_Q_RANK
_O_KRA = _O_CKV + MLA_KV_RANK
_O_KRB = _O_KRA + LANES
_O_REST = _O_KRB + LANES
_ODD_REST = MLA_WIDTH + 2 * POOL_WIDTH
_ODD_COLS = _O_REST + _ODD_REST
_HEAD_W = LANES


def _odd_in_kernel(x_ref, pos_ref, w_ref, qg_ref, kg_ref, wq_ref, wqs_ref, wkn_ref, wv_ref, frq_ref,
                   sgn_ref, vone_ref, q_ref, k_ref, v_ref, r_ref):
    proj = _dot(x_ref[...].astype(BF16), w_ref[...])
    r_ref[...] = proj[:, _O_REST:].astype(BF16)
    ang = pos_ref[...].astype(F32) * frq_ref[...]
    cosf = jnp.cos(ang)
    sinf = jnp.sin(ang) * sgn_ref[...]
    cqn = _rms_norm_rows(proj[:, _O_CQ:_O_CQ + MLA_Q_RANK], qg_ref[...]).astype(BF16)
    qa = _dot(cqn, wq_ref[...])
    qb = _dot(cqn, wqs_ref[...])
    ckvn = _rms_norm_rows(proj[:, _O_CKV:_O_CKV + MLA_KV_RANK], kg_ref[...]).astype(BF16)
    kn = _dot(ckvn, wkn_ref[...])
    vv = _dot(ckvn, wv_ref[...]) + vone_ref[...]
    kr = proj[:, _O_KRA:_O_KRA + LANES] * cosf + proj[:, _O_KRB:_O_KRB + LANES] * sinf
    for h in range(MLA_HEADS):
        sl = slice(h * _HEAD_W, (h + 1) * _HEAD_W)
        q_ref[h] = ((qa[:, sl] * cosf + qb[:, sl] * sinf) * (ATTN_SCALE * LOG2E)).astype(BF16)
        k_ref[h] = (kn[:, sl] + kr).astype(BF16)
        v_ref[h] = vv[:, sl].astype(BF16)


def _odd_in(x2d, pos2d, w2, qg, kg, wq, wqs, wkn, wv, frq, sgn, vone, s, tm):
    t = x2d.shape[0]
    b = t // s
    tps = s // tm
    const = lambda i: (0, 0)
    hw = MLA_HEADS * _HEAD_W
    qkv_spec = pl.BlockSpec((None, MLA_HEADS, tm, _HEAD_W), lambda i: (i // tps, 0, i % tps, 0))
    qkv_shape = jax.ShapeDtypeStruct((b, MLA_HEADS, s, _HEAD_W), BF16)
    return pl.pallas_call(
        _odd_in_kernel,
        grid=(t // tm,),
        in_specs=[
            pl.BlockSpec((tm, D_MODEL), lambda i: (i, 0)),
            pl.BlockSpec((tm, 1), lambda i: (i, 0)),
            pl.BlockSpec((D_MODEL, _ODD_COLS), const),
            pl.BlockSpec((1, MLA_Q_RANK), const),
            pl.BlockSpec((1, MLA_KV_RANK), const),
            pl.BlockSpec((MLA_Q_RANK, hw), const),
            pl.BlockSpec((MLA_Q_RANK, hw), const),
            pl.BlockSpec((MLA_KV_RANK, hw), const),
            pl.BlockSpec((MLA_KV_RANK, hw), const),
            pl.BlockSpec((1, LANES), const),
            pl.BlockSpec((1, LANES), const),
            pl.BlockSpec((1, hw), const),
        ],
        out_specs=[qkv_spec, qkv_spec, qkv_spec, pl.BlockSpec((tm, _ODD_REST), lambda i: (i, 0))],
        out_shape=[qkv_shape, qkv_shape, qkv_shape, jax.ShapeDtypeStruct((t, _ODD_REST), BF16)],
        compiler_params=pltpu.CompilerParams(
            dimension_semantics=("parallel",), vmem_limit_bytes=VMEM_LIMIT),
        name="odd_in_proj_qkv",
    )(x2d, pos2d, w2, qg, kg, wq, wqs, wkn, wv, frq, sgn, vone)


def _attn_kernel(q_ref, k_ref, v_ref, o_ref, s_ref, p_ref, m_ref, a_ref, acc_ref, *, tq, tk, strip, unroll):
    seq = k_ref.shape[0]
    nq = seq // tq
    nk = seq // tk
    ipq = nk // unroll
    ncol = tk // LANES
    odd = (pl.program_id(1) % 2) == 1

    def scores(qi, kt, slot):
        qo = pl.multiple_of(qi * tq, tq)
        ko = pl.multiple_of(kt * tk, tk)
        s_ref[slot] = lax.dot_general(q_ref[pl.ds(qo, tq), :], k_ref[pl.ds(ko, tk), :],
                                      (((1,), (1,)), ((), ())), preferred_element_type=F32)

    def accumulate(kt, slot):
        vo = pl.multiple_of(kt * tk, tk)
        acc_ref[...] = acc_ref[...] * a_ref[...] + _dot(p_ref[slot], v_ref[pl.ds(vo, tk), :])

    def emit(qi):
        acc = acc_ref[...]
        denom = jnp.where(odd, acc[:, 0:1], acc[:, MLA_V:MLA_V + 1])
        o_ref[pl.ds(pl.multiple_of(qi * tq, tq), tq), :] = (acc / denom).astype(BF16)

    def softmax(slot, restart):
        for r in range(tq // strip):
            rows = slice(r * strip, (r + 1) * strip)
            blks = [s_ref[slot, rows, c * LANES:(c + 1) * LANES] for c in range(ncol)]
            mx = blks[0]
            for c in range(1, ncol):
                mx = jnp.maximum(mx, blks[c])
            m_old = m_ref[rows, :]
            if restart is not None:
                m_old = jnp.where(restart, -jnp.inf, m_old)
            m_new = jnp.maximum(m_old, jnp.max(mx, axis=1, keepdims=True))
            m_ref[rows, :] = m_new
            a_ref[rows, :] = jnp.exp2(m_old - m_new)
            for c in range(ncol):
                p_ref[slot, rows, c * LANES:(c + 1) * LANES] = jnp.exp2(blks[c] - m_new).astype(BF16)

    m_ref[...] = jnp.zeros(m_ref.shape, F32)
    a_ref[...] = jnp.ones(a_ref.shape, F32)
    acc_ref[...] = jnp.ones(acc_ref.shape, F32)
    p_ref[1] = jnp.zeros(p_ref.shape[1:], BF16)
    scores(0, 0, 0)

    def body(j, carry):
        qi = j // ipq
        jj = j % ipq
        first = jj == 0
        for u in range(unroll):
            slot = u % 2
            t = jj * unroll + u
            if u == 0:
                accumulate(jnp.where(first, nk - 1, t - 1), 1 - slot)
                emit(jnp.where(first, jnp.maximum(qi - 1, 0), qi))
            else:
                accumulate(t - 1, 1 - slot)
            if u == unroll - 1:
                last = jj == ipq - 1
                scores(jnp.where(last, jnp.minimum(qi + 1, nq - 1), qi), jnp.where(last, 0, t + 1), 1 - slot)
            else:
                scores(qi, t + 1, 1 - slot)
            softmax(slot, first if u == 0 else None)
        return carry

    lax.fori_loop(0, nq * ipq, body, 0)
    accumulate(nk - 1, 1 - unroll % 2)
    emit(nq - 1)


def _attention(q, k, v, tq, tk, strip, unroll):
    b, h, s, w = q.shape
    assert unroll % 2 == 0 and (s // tk) % unroll == 0 and s % tq == 0 and tq % strip == 0
    spec = pl.BlockSpec((None, None, s, w), lambda bi, hi: (bi, hi, 0, 0))
    return pl.pallas_call(
        functools.partial(_attn_kernel, tq=tq, tk=tk, strip=strip, unroll=unroll),
        grid=(b, h),
        in_specs=[spec, spec, spec],
        out_specs=spec,
        out_shape=jax.ShapeDtypeStruct((b, h, s, w), BF16),
        scratch_shapes=[
            pltpu.VMEM((2, tq, tk), F32),
            pltpu.VMEM((2, tq, tk), BF16),
            pltpu.VMEM((tq, LANES), F32),
            pltpu.VMEM((tq, LANES), F32),
            pltpu.VMEM((tq, w), F32),
        ],
        compiler_params=pltpu.CompilerParams(
            dimension_semantics=("parallel", "parallel"), vmem_limit_bytes=VMEM_LIMIT),
        name="mla_attention",
    )(q, k, v)


def _odd_out_kernel(o_ref, gc_ref, um_ref, up_ref, un_ref, gd_ref, x_ref, pw_ref, ps_ref, wc_ref, wd_ref,
                    lg_ref, lb_ref, out_ref, ext_ref, *, tiles_per_seq, seq):
    tm = x_ref.shape[0]
    i = pl.program_id(0) % tiles_per_seq
    lane = lax.broadcasted_iota(jnp.int32, (tm, LANES), 1)
    low = lane < MLA_V
    gc = _silu(gc_ref[...].astype(F32))
    acc = None
    for j in range(MLA_HEADS // 2):
        pair = jnp.where(low, o_ref[2 * j], o_ref[2 * j + 1]).astype(F32)
        yc = (pair * gc[:, j * LANES:(j + 1) * LANES]).astype(BF16)
        t = _dot(yc, wc_ref[j * LANES:(j + 1) * LANES, :])
        acc = t if acc is None else acc + t

    um = um_ref[...].astype(F32)
    ext_ref[HALO:HALO + tm, :] = um
    ext_ref[0:HALO, :] = jnp.where(i > 0, up_ref[...].astype(F32), 0.0)
    ext_ref[HALO + tm:2 * HALO + tm, :] = jnp.where(i < tiles_per_seq - 1, un_ref[...].astype(F32), 0.0)
    pos = i * tm + lax.broadcasted_iota(jnp.int32, (tm, LANES), 0)
    gd = _silu(gd_ref[...].astype(F32))
    for gi, w in enumerate(POOL_WINDOWS):
        cs = slice(gi * POOL_GROUP, (gi + 1) * POOL_GROUP)
        wsum = None
        for d in range(-(w // 2), w - w // 2):
            term = ext_ref[HALO + d:HALO + d + tm, cs]
            wsum = term if wsum is None else wsum + term
        cnt = (jnp.minimum(pos + (w - w // 2), seq) - jnp.maximum(pos - w // 2, 0)).astype(F32)
        pooled = wsum / cnt - um[:, cs]
        yd = _dot(pooled.astype(BF16), pw_ref[gi]) * ps_ref[:, cs] * gd[:, cs]
        acc = acc + _dot(yd.astype(BF16), wd_ref[cs, :])
    out_ref[...] = _layer_norm_rows(ALPHA * x_ref[...] + acc, lg_ref[...], lb_ref[...])


def _odd_out(o, rest, x2d, pool_w, pool_scale, w_c, w_d, ln_g, ln_b, s, tm):
    t = x2d.shape[0]
    tps = s // tm
    hb = tm // HALO
    nh = t // HALO
    const = lambda i: (0, 0)
    pw = POOL_WIDTH
    return pl.pallas_call(
        functools.partial(_odd_out_kernel, tiles_per_seq=tps, seq=s),
        grid=(t // tm,),
        in_specs=[
            pl.BlockSpec((None, MLA_HEADS, tm, _HEAD_W), lambda i: (i // tps, 0, i % tps, 0)),
            pl.BlockSpec((tm, pw), lambda i: (i, 0)),
            pl.BlockSpec((tm, pw), lambda i: (i, 1)),
            pl.BlockSpec((HALO, pw), lambda i: (jnp.maximum(i * hb - 1, 0), 1)),
            pl.BlockSpec((HALO, pw), lambda i: (jnp.minimum((i + 1) * hb, nh - 1), 1)),
            pl.BlockSpec((tm, pw), lambda i: (i, 2)),
            pl.BlockSpec((tm, D_MODEL), lambda i: (i, 0)),
            pl.BlockSpec((len(POOL_WINDOWS), POOL_GROUP, POOL_GROUP), lambda i: (0, 0, 0)),
            pl.BlockSpec((1, pw), const),
            pl.BlockSpec((MLA_WIDTH, D_MODEL), const),
            pl.BlockSpec((pw, D_MODEL), const),
            pl.BlockSpec((1, D_MODEL), const),
            pl.BlockSpec((1, D_MODEL), const),
        ],
        out_specs=pl.BlockSpec((tm, D_MODEL), lambda i: (i, 0)),
        out_shape=jax.ShapeDtypeStruct((t, D_MODEL), F32),
        scratch_shapes=[pltpu.VMEM((tm + 2 * HALO, pw), F32)],
        compiler_params=pltpu.CompilerParams(
            dimension_semantics=("parallel",), vmem_limit_bytes=VMEM_LIMIT),
        name="odd_out_proj",
    )(o, rest, rest, rest, rest, rest, x2d, pool_w, pool_scale, w_c, w_d, ln_g, ln_b)


def _rep3(v):
    z = jnp.zeros((LANES - 3 * SSD_HEADS,), F32)
    return jnp.concatenate([v, v, v, z])[None, :]


def _even_params(w_in, conv_w, conv_b, a_log, dt_bias, d_skip, norm_g, sc_conv_w, w_out):
    o_z = 0
    o_xbc = SSD_INNER
    o_dt = o_xbc + SSD_XBC
    o_sc = o_dt + 2 * SSD_HEADS
    w_main = jnp.concatenate([w_in[:, o_xbc:o_dt], w_in[:, o_z:o_xbc], w_in[:, o_sc:]], axis=1).astype(BF16)
    zpad = jnp.zeros((D_MODEL, LANES - 3 * SSD_HEADS), F32)
    wdt = []
    for k in range(2):
        wk = w_in[:, o_dt + k * SSD_HEADS:o_dt + (k + 1) * SSD_HEADS]
        wdt.append(jnp.concatenate([wk, wk, wk, zpad], axis=1))
    wdt = jnp.concatenate(wdt, axis=1)
    wdt_hi = wdt.astype(BF16)
    wdt_lo = (wdt - wdt_hi.astype(F32)).astype(BF16)
    a_rows = [_rep3(-jnp.exp(a_log[k].astype(F32)) * LOG2E) for k in range(2)]
    dtb_rows = [_rep3(dt_bias[k].astype(F32)) for k in range(2)]
    dsk_rows = [jnp.repeat(d_skip[k].astype(F32), SSD_HEAD_DIM)[None, :] for k in range(2)]
    src = jnp.arange(LANES)[:, None]
    dst = jnp.arange(SSD_INNER)[None, :] // SSD_HEAD_DIM
    e_mat = jnp.where((src < 2 * SSD_HEADS) & (src % SSD_HEADS == dst), 1.0, 0.0).astype(BF16)
    trow = jnp.arange(SSD_CHUNK)[:, None]
    wcol = jnp.arange(SSD_CHUNK + 2 * HALO_B)[None, :]
    shifts = jnp.stack([jnp.where(wcol == trow + HALO_B + (k - SSD_CONV_LEFT), 1.0, 0.0)
                        for k in range(SSD_CONV) if k != SSD_CONV_LEFT]).astype(BF16)
    return dict(shifts=shifts,
        w_main=w_main, wdt_hi=wdt_hi, wdt_lo=wdt_lo, conv_w=conv_w.astype(F32), conv_b=conv_b[None, :].astype(F32),
        a_rows=a_rows, dtb_rows=dtb_rows, dsk_rows=dsk_rows, e_mat=e_mat, norm_g=norm_g[None, :].astype(F32),
        sc_w=sc_conv_w.astype(F32), w_a=w_out[:SSD_INNER].astype(BF16), w_b=w_out[SSD_INNER:].astype(BF16))


def _odd_params(w_in, q_norm_g, w_uq, kv_norm_g, w_ukv, pool_w, pool_scale, w_out):
    o_ckv = MLA_Q_RANK
    o_kr = o_ckv + MLA_KV_RANK
    o_rest = o_kr + MLA_ROPE
    half = MLA_ROPE // 2
    z64 = jnp.zeros((D_MODEL, MLA_NOPE), F32)
    z32 = jnp.zeros((D_MODEL, LANES - MLA_NOPE - MLA_ROPE), F32)
    kr1 = w_in[:, o_kr:o_kr + half]
    kr2 = w_in[:, o_kr + half:o_kr + MLA_ROPE]
    w2 = jnp.concatenate([w_in[:, :o_kr], z64, kr1, kr2, z32, z64, kr2, kr1, z32, w_in[:, o_rest:]],
                         axis=1).astype(BF16)
    qd = MLA_NOPE + MLA_ROPE
    w3 = w_uq.reshape(MLA_Q_RANK, MLA_HEADS, qd)
    zq = lambda n: jnp.zeros((MLA_Q_RANK, MLA_HEADS, n), F32)
    wq = jnp.concatenate([w3, zq(_HEAD_W - qd)], axis=-1).reshape(MLA_Q_RANK, -1).astype(BF16)
    wqs = jnp.concatenate([zq(MLA_NOPE), w3[..., MLA_NOPE + half:], w3[..., MLA_NOPE:MLA_NOPE + half],
                           zq(_HEAD_W - qd)], axis=-1).reshape(MLA_Q_RANK, -1).astype(BF16)
    w4 = w_ukv.reshape(MLA_KV_RANK, MLA_HEADS, MLA_NOPE + MLA_V)
    zk = jnp.zeros((MLA_KV_RANK, MLA_HEADS, _HEAD_W - MLA_NOPE), F32)
    wkn = jnp.concatenate([w4[..., :MLA_NOPE], zk], axis=-1).reshape(MLA_KV_RANK, -1).astype(BF16)
    zv = jnp.zeros((MLA_KV_RANK, _HEAD_W - MLA_V), F32)
    wv_blocks = []
    vone = []
    for h in range(MLA_HEADS):
        vh = w4[:, h, MLA_NOPE:]
        wv_blocks.append(jnp.concatenate([vh, zv] if h % 2 == 0 else [zv, vh], axis=-1))
        vone.append(jnp.zeros((_HEAD_W,), F32).at[MLA_V if h % 2 == 0 else 0].set(1.0))
    wv = jnp.concatenate(wv_blocks, axis=-1).astype(BF16)
    vone = jnp.concatenate(vone)[None, :]
    inv_freq = ROPE_THETA ** (-jnp.arange(half, dtype=F32) / half)
    zl = lambda n: jnp.zeros((n,), F32)
    frq = jnp.concatenate([zl(MLA_NOPE), inv_freq, inv_freq, zl(LANES - qd)])[None, :]
    sgn = jnp.concatenate([zl(MLA_NOPE), -jnp.ones((half,), F32), jnp.ones((half,), F32), zl(LANES - qd)])[None, :]
    return dict(
        w2=w2, qg=q_norm_g[None, :].astype(F32), kg=kv_norm_g[None, :].astype(F32), wq=wq, wqs=wqs, wkn=wkn,
        wv=wv, vone=vone, frq=frq, sgn=sgn, pool_w=pool_w.astype(BF16), pool_scale=pool_scale[None, :].astype(F32),
        w_c=w_out[:MLA_WIDTH].astype(BF16), w_d=w_out[MLA_WIDTH:].astype(BF16))


def _pick(n, pref):
    t = min(pref, n)
    while n % t:
        t //= 2
    return t


def _even_layer(x2d, s, p, ln_g, ln_b):
    t = x2d.shape[0]
    proj, dt_raw = _even_in_proj(x2d, p["w_main"], p["wdt_hi"], p["wdt_lo"], _pick(t, 1024), 3584)
    yb, xact = _ssd_backward_sweep(proj, dt_raw, p["shifts"], p["conv_w"], p["conv_b"], p["a_rows"][1],
                                   p["dtb_rows"][1], p["dsk_rows"][1], p["e_mat"], s)
    ya = _ssd_forward_sweep(xact, dt_raw, proj, yb, p["a_rows"][0], p["dtb_rows"][0], p["dsk_rows"][0],
                            p["e_mat"], p["norm_g"], s)
    return _even_out(ya, proj, x2d, p["sc_w"], p["w_a"], p["w_b"], ln_g[None, :], ln_b[None, :], s, _pick(s, 512))


def _odd_layer(x2d, pos2d, s, p, ln_g, ln_b):
    tm = _pick(s, 512)
    q, k, v, rest = _odd_in(x2d, pos2d, p["w2"], p["qg"], p["kg"], p["wq"], p["wqs"], p["wkn"], p["wv"],
                            p["frq"], p["sgn"], p["vone"], s, tm)
    o = _attention(q, k, v, _pick(s, 512), _pick(s // 16, 512), 64, 16)
    return _odd_out(o, rest, x2d, p["pool_w"], p["pool_scale"], p["w_c"], p["w_d"], ln_g[None, :], ln_b[None, :],
                    s, tm)


def kernel(x, positions, ev_w_in, ev_conv_w, ev_conv_b, ev_a_log, ev_dt_bias, ev_d_skip, ev_norm_g, ev_sc_conv_w,
           ev_w_out, ev_ln_g, ev_ln_b, od_w_in, od_q_norm_g, od_w_uq, od_kv_norm_g, od_w_ukv, od_pool_w,
           od_pool_scale, od_w_out, od_ln_g, od_ln_b):
    b, s, d = x.shape
    x2d = x.reshape(b * s, d)
    pos2d = positions.reshape(b * s, 1)
    for layer in range(DEPTH):
        i = layer // 2
        if layer % 2 == 0:
            p = _even_params(ev_w_in[i], ev_conv_w[i], ev_conv_b[i], ev_a_log[i], ev_dt_bias[i], ev_d_skip[i],
                             ev_norm_g[i], ev_sc_conv_w[i], ev_w_out[i])
            x2d = _even_layer(x2d, s, p, ev_ln_g[i], ev_ln_b[i])
        else:
            p = _odd_params(od_w_in[i], od_q_norm_g[i], od_w_uq[i], od_kv_norm_g[i], od_w_ukv[i], od_pool_w[i],
                            od_pool_scale[i], od_w_out[i])
            x2d = _odd_layer(x2d, pos2d, s, p, od_ln_g[i], od_ln_b[i])
    return x2d.reshape(b, s, d)
```

```python
import functools
import math

import jax
import jax.numpy as jnp
from jax import lax
from jax.experimental import pallas as pl
from jax.experimental.pallas import tpu as pltpu

F32 = jnp.float32
BF16 = jnp.bfloat16

D_MODEL = 1024
DEPTH = 2
SSD_HEADS = 16
SSD_HEAD_DIM = 64
SSD_INNER = SSD_HEADS * SSD_HEAD_DIM
SSD_GROUPS = 4
SSD_STATE = 128
SSD_CONV = 4
SSD_CONV_LEFT = 2
SSD_CHUNK = 128
SSD_STEP = 4
SSD_XBC = SSD_INNER + 2 * SSD_GROUPS * SSD_STATE
SC_WIDTH = 1024
MLA_HEADS = 8
MLA_Q_RANK = 256
MLA_KV_RANK = 128
MLA_NOPE = 64
MLA_ROPE = 32
MLA_V = 64
MLA_WIDTH = MLA_HEADS * MLA_V
ROPE_THETA = 10000.0
ATTN_SCALE = (MLA_NOPE + MLA_ROPE) ** -0.5
LOG2E = math.log2(math.e)
POOL_WINDOWS = (2, 4, 8, 16)
POOL_GROUP = 128
POOL_WIDTH = POOL_GROUP * len(POOL_WINDOWS)
EPS = 1e-5
ALPHA = (2 * DEPTH) ** 0.25

LANES = 128
HALO = 8
HALO_B = 16
VMEM_LIMIT = 56 * 1024 * 1024


def _dot(a, b):
    return jnp.dot(a, b, preferred_element_type=F32)


def _silu(v):
    h = 0.5 * v
    return h + h * jnp.tanh(h)


def _softplus(v):
    return jnp.maximum(v, 0.0) + jnp.log1p(jnp.exp(-jnp.abs(v)))


def _split_bf16(v, pieces):
    out = []
    r = v
    for _ in range(pieces):
        p = r.astype(BF16)
        out.append(p)
        r = r - p.astype(F32)
    return out


def _layer_norm_rows(r, g, b):
    mu = jnp.mean(r, axis=-1, keepdims=True)
    d = r - mu
    var = jnp.mean(d * d, axis=-1, keepdims=True)
    return d * lax.rsqrt(var + EPS) * g + b


def _rms_norm_rows(v, g):
    return v * lax.rsqrt(jnp.mean(v * v, axis=-1, keepdims=True) + EPS) * g


def _even_in_proj_kernel(x_ref, w_ref, wdh_ref, wdl_ref, o_ref, dt_ref, xb_ref):
    @pl.when(pl.program_id(1) == 0)
    def _():
        x = x_ref[...]
        xh = x.astype(BF16)
        xb_ref[...] = xh
        xl = (x - xh.astype(F32)).astype(BF16)
        wh = wdh_ref[...]
        dt_ref[...] = _dot(xh, wh) + _dot(xl, wh) + _dot(xh, wdl_ref[...])

    o_ref[...] = _dot(xb_ref[...], w_ref[...]).astype(BF16)


def _even_in_proj(x2d, w_main, wdt_hi, wdt_lo, tm, tn):
    t, k = x2d.shape
    n = w_main.shape[1]
    ndt = wdt_hi.shape[1]
    return pl.pallas_call(
        _even_in_proj_kernel,
        grid=(t // tm, n // tn),
        in_specs=[
            pl.BlockSpec((tm, k), lambda i, j: (i, 0)),
            pl.BlockSpec((k, tn), lambda i, j: (0, j)),
            pl.BlockSpec((k, ndt), lambda i, j: (0, 0)),
            pl.BlockSpec((k, ndt), lambda i, j: (0, 0)),
        ],
        out_specs=[
            pl.BlockSpec((tm, tn), lambda i, j: (i, j)),
            pl.BlockSpec((tm, ndt), lambda i, j: (i, 0)),
        ],
        out_shape=[jax.ShapeDtypeStruct((t, n), BF16), jax.ShapeDtypeStruct((t, ndt), F32)],
        scratch_shapes=[pltpu.VMEM((tm, k), BF16)],
        compiler_params=pltpu.CompilerParams(
            dimension_semantics=("parallel", "arbitrary"), vmem_limit_bytes=VMEM_LIMIT),
        name="even_in_proj",
    )(x2d, w_main, wdt_hi, wdt_lo)


def _conv_silu(first, last, xm_ref, xp_ref, xn_ref, sh_ref, cw_ref, cb_ref, xact_ref):
    rows = xm_ref.shape[0]
    q = SSD_CHUNK
    blk = 2 * LANES
    taps = [k for k in range(SSD_CONV) if k != SSD_CONV_LEFT]
    for j in range(SSD_XBC // blk):
        sl = slice(j * blk, (j + 1) * blk)
        xp = xp_ref[:, sl]
        xn = xn_ref[:, sl]
        ext = jnp.concatenate([jnp.where(first, jnp.zeros_like(xp), xp), xm_ref[:, sl],
                               jnp.where(last, jnp.zeros_like(xn), xn)], axis=0)
        for r0 in range(0, rows, q):
            win = ext[r0:r0 + q + 2 * HALO_B]
            acc = cb_ref[:, sl] + cw_ref[SSD_CONV_LEFT:SSD_CONV_LEFT + 1, sl] * xm_ref[r0:r0 + q, sl].astype(F32)
            for i, k in enumerate(taps):
                acc = acc + cw_ref[k:k + 1, sl] * _dot(sh_ref[i], win)
            xact_ref[r0:r0 + q, sl] = _silu(acc).astype(BF16)


def _ssd_chunk(reverse, r0, xact_ref, dt_ref, a_ref, dtb_ref, dsk_ref, e_ref, h_ref, emit):
    q = SSD_CHUNK
    rs = slice(r0, r0 + q)

    dt = _softplus(dt_ref[rs, :] + dtb_ref[...])
    da = dt * a_ref[...]
    row = lax.broadcasted_iota(jnp.int32, (q, q), 0)
    col = lax.broadcasted_iota(jnp.int32, (q, q), 1)
    keep = (col >= row) if reverse else (col <= row)
    tri = jnp.where(keep, 1.0, 0.0).astype(BF16)
    p1, p2, p3 = _split_bf16(da, 3)
    cs = _dot(tri, p1) + _dot(tri, p2) + _dot(tri, p3)
    cs_t = cs.T
    end = 0 if reverse else q - 1
    ecs = jnp.exp2(cs)
    dec = jnp.exp2(cs[end:end + 1, :] - cs)

    lane = lax.broadcasted_iota(jnp.int32, (q, LANES), 1)

    def pieces(v):
        hi, lo = _split_bf16(v, 2)
        return jnp.where(lane < SSD_HEADS, hi, lo)

    dt_p, ecs_p, dec_p = pieces(dt), pieces(ecs), pieces(dec)
    half = lane < SSD_HEAD_DIM
    gw = (SSD_HEADS // SSD_GROUPS) * SSD_HEAD_DIM
    o_b = SSD_INNER
    o_c = SSD_INNER + SSD_GROUPS * SSD_STATE

    for g in range(SSD_GROUPS):
        gs = slice(g * gw, (g + 1) * gw)
        e_g = e_ref[:, gs]
        dt_g = _dot(dt_p, e_g)
        ecs_g = _dot(ecs_p, e_g)
        dec_g = _dot(dec_p, e_g)
        xs_g = xact_ref[rs, gs].astype(F32)
        xd_g = xs_g * dt_g
        xd_b = xd_g.astype(BF16)
        xdd_b = (xd_g * dec_g).astype(BF16)
        cg_b = xact_ref[rs, o_c + g * SSD_STATE:o_c + (g + 1) * SSD_STATE]
        bgt_b = xact_ref[rs, o_b + g * SSD_STATE:o_b + (g + 1) * SSD_STATE].astype(F32).T.astype(BF16)
        cbm = _dot(cg_b, bgt_b)
        h_in = h_ref[g]
        y_off = _dot(cg_b, h_in.astype(BF16))
        st = _dot(bgt_b, xdd_b)
        h_ref[g] = h_in * ecs_g[end:end + 1, :] + st
        for jj in range(gw // LANES):
            js = slice(jj * LANES, (jj + 1) * LANES)
            cidx = g * gw + jj * LANES
            xd_blk = xd_b[:, js]
            yd = None
            for hh in range(2):
                h = (cidx // SSD_HEAD_DIM) + hh
                diff = cs[:, h:h + 1] - cs_t[h:h + 1, :]
                m = (cbm * jnp.where(keep, jnp.exp2(diff), 0.0)).astype(BF16)
                rhs = jnp.where(half if hh == 0 else jnp.logical_not(half), xd_blk, jnp.zeros_like(xd_blk))
                t = _dot(m, rhs)
                yd = t if yd is None else yd + t
            y_blk = yd + y_off[:, js] * ecs_g[:, js] + dsk_ref[:, cidx:cidx + LANES] * xs_g[:, js]
            emit(r0, cidx, y_blk)


def _reset_state(h_ref):
    @pl.when(pl.program_id(1) == 0)
    def _():
        h_ref[...] = jnp.zeros_like(h_ref)


def _ssd_bwd_kernel(xm_ref, xp_ref, xn_ref, sh_ref, dt_ref, cw_ref, cb_ref, a_ref, dtb_ref, dsk_ref, e_ref,
                    y_ref, xact_ref, h_ref):
    nt = pl.num_programs(1)
    tile = nt - 1 - pl.program_id(1)
    _conv_silu(tile == 0, tile == nt - 1, xm_ref, xp_ref, xn_ref, sh_ref, cw_ref, cb_ref, xact_ref)
    _reset_state(h_ref)

    def emit(r0, cidx, y_blk):
        y_ref[r0:r0 + SSD_CHUNK, cidx:cidx + LANES] = y_blk

    for sub in reversed(range(SSD_STEP)):
        _ssd_chunk(True, sub * SSD_CHUNK, xact_ref, dt_ref, a_ref, dtb_ref, dsk_ref, e_ref, h_ref, emit)


def _ssd_fwd_kernel(xact_ref, dt_ref, a_ref, dtb_ref, dsk_ref, e_ref, z_ref, yb_ref, ng_ref,
                    y_ref, h_ref, yz_ref):
    _reset_state(h_ref)

    def emit(r0, cidx, y_blk):
        rs = slice(r0, r0 + SSD_CHUNK)
        z = z_ref[rs, cidx:cidx + LANES].astype(F32)
        yz_ref[rs, cidx:cidx + LANES] = (y_blk + yb_ref[rs, cidx:cidx + LANES]) * _silu(z)

    for sub in range(SSD_STEP):
        _ssd_chunk(False, sub * SSD_CHUNK, xact_ref, dt_ref, a_ref, dtb_ref, dsk_ref, e_ref, h_ref, emit)
    y_ref[...] = _rms_norm_rows(yz_ref[...], ng_ref[...]).astype(BF16)


def _ssd_backward_sweep(proj, dt_raw, shifts, conv_w, conv_b, a_row, dtb_row, dsk_row, e_mat, s):
    t = proj.shape[0]
    rows = SSD_STEP * SSD_CHUNK
    nt = s // rows
    hb = rows // HALO_B
    nhb = s // HALO_B
    tl = lambda c: nt - 1 - c
    main = lambda bi, c: (bi * nt + tl(c), 0)
    prev = lambda bi, c: (bi * nhb + jnp.maximum(tl(c) * hb - 1, 0), 0)
    nxt = lambda bi, c: (bi * nhb + jnp.minimum((tl(c) + 1) * hb, nhb - 1), 0)
    const = lambda bi, c: (0, 0)
    return pl.pallas_call(
        _ssd_bwd_kernel,
        grid=(t // s, nt),
        in_specs=[
            pl.BlockSpec((rows, SSD_XBC), main),
            pl.BlockSpec((HALO_B, SSD_XBC), prev),
            pl.BlockSpec((HALO_B, SSD_XBC), nxt),
            pl.BlockSpec(shifts.shape, lambda bi, c: (0, 0, 0)),
            pl.BlockSpec((rows, LANES), lambda bi, c: (bi * nt + tl(c), 1)),
            pl.BlockSpec((SSD_CONV, SSD_XBC), const),
            pl.BlockSpec((1, SSD_XBC), const),
            pl.BlockSpec((1, LANES), const),
            pl.BlockSpec((1, LANES), const),
            pl.BlockSpec((1, SSD_INNER), const),
            pl.BlockSpec((LANES, SSD_INNER), const),
        ],
        out_specs=[pl.BlockSpec((rows, SSD_INNER), main), pl.BlockSpec((rows, SSD_XBC), main)],
        out_shape=[jax.ShapeDtypeStruct((t, SSD_INNER), F32), jax.ShapeDtypeStruct((t, SSD_XBC), BF16)],
        scratch_shapes=[pltpu.VMEM((SSD_GROUPS, SSD_STATE, SSD_INNER // SSD_GROUPS), F32)],
        compiler_params=pltpu.CompilerParams(
            dimension_semantics=("parallel", "arbitrary"), vmem_limit_bytes=VMEM_LIMIT),
        name="ssd_backward_sweep",
    )(proj, proj, proj, shifts, dt_raw, conv_w, conv_b, a_row, dtb_row, dsk_row, e_mat)


def _ssd_forward_sweep(xact, dt_raw, proj, yb, a_row, dtb_row, dsk_row, e_mat, norm_g, s):
    t = xact.shape[0]
    rows = SSD_STEP * SSD_CHUNK
    nt = s // rows
    main = lambda bi, c: (bi * nt + c, 0)
    const = lambda bi, c: (0, 0)
    return pl.pallas_call(
        _ssd_fwd_kernel,
        grid=(t // s, nt),
        in_specs=[
            pl.BlockSpec((rows, SSD_XBC), main),
            pl.BlockSpec((rows, LANES), main),
            pl.BlockSpec((1, LANES), const),
            pl.BlockSpec((1, LANES), const),
            pl.BlockSpec((1, SSD_INNER), const),
            pl.BlockSpec((LANES, SSD_INNER), const),
            pl.BlockSpec((rows, SSD_INNER), lambda bi, c: (bi * nt + c, 2)),
            pl.BlockSpec((rows, SSD_INNER), main),
            pl.BlockSpec((1, SSD_INNER), const),
        ],
        out_specs=pl.BlockSpec((rows, SSD_INNER), main),
        out_shape=jax.ShapeDtypeStruct((t, SSD_INNER), BF16),
        scratch_shapes=[pltpu.VMEM((SSD_GROUPS, SSD_STATE, SSD_INNER // SSD_GROUPS), F32),
                        pltpu.VMEM((rows, SSD_INNER), F32)],
        compiler_params=pltpu.CompilerParams(
            dimension_semantics=("parallel", "arbitrary"), vmem_limit_bytes=VMEM_LIMIT),
        name="ssd_forward_sweep",
    )(xact, dt_raw, a_row, dtb_row, dsk_row, e_mat, proj, yb, norm_g)


def _even_out_kernel(ya_ref, bg_ref, cgm_ref, cgp_ref, cgn_ref, hm_ref, hp_ref, hn_ref, gt_ref, x_ref,
                     scw_ref, wa_ref, wb_ref, lg_ref, lb_ref, o_ref, yb_ref, *, tiles_per_seq):
    tm = ya_ref.shape[0]
    i = pl.program_id(0) % tiles_per_seq
    blk = 2 * LANES
    row = lax.broadcasted_iota(jnp.int32, (tm, blk), 0)
    for j in range(SC_WIDTH // blk):
        sl = slice(j * blk, (j + 1) * blk)
        ch = cgm_ref[:, sl].astype(F32) * hm_ref[:, sl].astype(F32)
        before = jnp.where(i > 0, cgp_ref[HALO - 1:HALO, sl].astype(F32) * hp_ref[HALO - 1:HALO, sl].astype(F32), 0.0)
        after = jnp.where(i < tiles_per_seq - 1, cgn_ref[0:1, sl].astype(F32) * hn_ref[0:1, sl].astype(F32), 0.0)
        up = jnp.where(row == 0, before, pltpu.roll(ch, 1, axis=0))
        dn = jnp.where(row == tm - 1, after, pltpu.roll(ch, tm - 1, axis=0))
        conv = scw_ref[0:1, sl] * up + scw_ref[1:2, sl] * ch + scw_ref[2:3, sl] * dn
        yb_ref[:, sl] = (bg_ref[:, sl].astype(F32) * conv * _silu(gt_ref[:, sl].astype(F32))).astype(BF16)
    h = _dot(ya_ref[...], wa_ref[...]) + _dot(yb_ref[...], wb_ref[...])
    o_ref[...] = _layer_norm_rows(ALPHA * x_ref[...] + h, lg_ref[...], lb_ref[...])


def _even_out(ya, proj, x2d, sc_w, w_a, w_b, ln_g, ln_b, s, tm):
    t = x2d.shape[0]
    tps = s // tm
    hb = tm // HALO
    nh = t // HALO

    def colblk(j):
        return lambda i: (i, j)

    def prev(j):
        return lambda i: (jnp.maximum(i * hb - 1, 0), j)

    def nxt(j):
        return lambda i: (jnp.minimum((i + 1) * hb, nh - 1), j)

    w = SC_WIDTH
    const = lambda i: (0, 0)
    return pl.pallas_call(
        functools.partial(_even_out_kernel, tiles_per_seq=tps),
        grid=(t // tm,),
        in_specs=[
            pl.BlockSpec((tm, w), colblk(0)),
            pl.BlockSpec((tm, w), colblk(3)),
            pl.BlockSpec((tm, w), colblk(4)), pl.BlockSpec((HALO, w), prev(4)), pl.BlockSpec((HALO, w), nxt(4)),
            pl.BlockSpec((tm, w), colblk(5)), pl.BlockSpec((HALO, w), prev(5)), pl.BlockSpec((HALO, w), nxt(5)),
            pl.BlockSpec((tm, w), colblk(6)),
            pl.BlockSpec((tm, D_MODEL), colblk(0)),
            pl.BlockSpec((3, w), const),
            pl.BlockSpec((SSD_INNER, D_MODEL), const),
            pl.BlockSpec((w, D_MODEL), const),
            pl.BlockSpec((1, D_MODEL), const),
            pl.BlockSpec((1, D_MODEL), const),
        ],
        out_specs=pl.BlockSpec((tm, D_MODEL), colblk(0)),
        out_shape=jax.ShapeDtypeStruct((t, D_MODEL), F32),
        scratch_shapes=[pltpu.VMEM((tm, w), BF16)],
        compiler_params=pltpu.CompilerParams(
            dimension_semantics=("parallel",), vmem_limit_bytes=VMEM_LIMIT),
        name="even_out_proj",
    )(ya, proj, proj, proj, proj, proj, proj, proj, proj, x2d, sc_w, w_a, w_b, ln_g, ln_b)


_O_CQ = 0
_O_CKV = MLA_Q_RANK
_O_KRA = _O_CKV + MLA_KV_RANK
_O_KRB = _O_KRA + LANES
_O_REST = _O_KRB + LANES
_ODD_REST = MLA_WIDTH + 2 * POOL_WIDTH
_ODD_COLS = _O_REST + _ODD_REST
_HEAD_W = LANES


def _odd_in_kernel(x_ref, pos_ref, w_ref, qg_ref, kg_ref, wq_ref, wqs_ref, wkn_ref, wv_ref, frq_ref,
                   sgn_ref, vone_ref, q_ref, k_ref, v_ref, r_ref):
    proj = _dot(x_ref[...].astype(BF16), w_ref[...])
    r_ref[...] = proj[:, _O_REST:].astype(BF16)
    ang = frq_ref[...] * pos_ref[...].astype(F32)
    pad = jnp.zeros((LANES - 3 * ang.shape[0], ang.shape[1]), F32)

    def table(v, place):
        hi, lo = _split_bf16(v, 2)
        stack = jnp.concatenate([hi.astype(F32), lo.astype(F32), jnp.ones_like(v), pad], axis=0)
        return _dot(stack.T.astype(BF16), place)

    cosf = table(jnp.cos(ang), sgn_ref[0])
    sinf = table(jnp.sin(ang), sgn_ref[1])
    cqn = _rms_norm_rows(proj[:, _O_CQ:_O_CQ + MLA_Q_RANK], qg_ref[...]).astype(BF16)
    qa = _dot(cqn, wq_ref[...])
    qb = _dot(cqn, wqs_ref[...])
    ckvn = _rms_norm_rows(proj[:, _O_CKV:_O_CKV + MLA_KV_RANK], kg_ref[...]).astype(BF16)
    kn = _dot(ckvn, wkn_ref[...])
    vv = _dot(ckvn, wv_ref[...]) + vone_ref[...]
    kr = proj[:, _O_KRA:_O_KRA + LANES] * cosf + proj[:, _O_KRB:_O_KRB + LANES] * sinf
    for h in range(MLA_HEADS):
        sl = slice(h * _HEAD_W, (h + 1) * _HEAD_W)
        q_ref[h] = ((qa[:, sl] * cosf + qb[:, sl] * sinf) * (ATTN_SCALE * LOG2E)).astype(BF16)
        k_ref[h] = (kn[:, sl] + kr).astype(BF16)
        v_ref[h] = vv[:, sl].astype(BF16)


def _odd_in(x2d, pos2d, w2, qg, kg, wq, wqs, wkn, wv, frq, sgn, vone, s, tm):
    t = x2d.shape[0]
    b = t // s
    tps = s // tm
    const = lambda i: (0, 0)
    hw = MLA_HEADS * _HEAD_W
    qkv_spec = pl.BlockSpec((None, MLA_HEADS, tm, _HEAD_W), lambda i: (i // tps, 0, i % tps, 0))
    qkv_shape = jax.ShapeDtypeStruct((b, MLA_HEADS, s, _HEAD_W), BF16)
    return pl.pallas_call(
        _odd_in_kernel,
        grid=(t // tm,),
        in_specs=[
            pl.BlockSpec((tm, D_MODEL), lambda i: (i, 0)),
            pl.BlockSpec((None, 1, tm), lambda i: (i, 0, 0)),
            pl.BlockSpec((D_MODEL, _ODD_COLS), const),
            pl.BlockSpec((1, MLA_Q_RANK), const),
            pl.BlockSpec((1, MLA_KV_RANK), const),
            pl.BlockSpec((MLA_Q_RANK, hw), const),
            pl.BlockSpec((MLA_Q_RANK, hw), const),
            pl.BlockSpec((MLA_KV_RANK, hw), const),
            pl.BlockSpec((MLA_KV_RANK, hw), const),
            pl.BlockSpec(frq.shape, const),
            pl.BlockSpec(sgn.shape, lambda i: (0, 0, 0)),
            pl.BlockSpec((1, hw), const),
        ],
        out_specs=[qkv_spec, qkv_spec, qkv_spec, pl.BlockSpec((tm, _ODD_REST), lambda i: (i, 0))],
        out_shape=[qkv_shape, qkv_shape, qkv_shape, jax.ShapeDtypeStruct((t, _ODD_REST), BF16)],
        compiler_params=pltpu.CompilerParams(
            dimension_semantics=("parallel",), vmem_limit_bytes=VMEM_LIMIT),
        name="odd_in_proj_qkv",
    )(x2d, pos2d, w2, qg, kg, wq, wqs, wkn, wv, frq, sgn, vone)


def _attn_kernel(q_ref, k_ref, v_ref, o_ref, s_ref, p_ref, m_ref, a_ref, acc_ref, *, tq, tk, strip, unroll):
    seq = k_ref.shape[0]
    nq = seq // tq
    nk = seq // tk
    ipq = nk // unroll
    ncol = tk // LANES
    odd = (pl.program_id(1) % 2) == 1

    def scores(qi, kt, slot):
        qo = pl.multiple_of(qi * tq, tq)
        ko = pl.multiple_of(kt * tk, tk)
        s_ref[slot] = lax.dot_general(q_ref[pl.ds(qo, tq), :], k_ref[pl.ds(ko, tk), :],
                                      (((1,), (1,)), ((), ())), preferred_element_type=F32)

    def accumulate(kt, slot):
        vo = pl.multiple_of(kt * tk, tk)
        acc_ref[...] = acc_ref[...] * a_ref[...] + _dot(p_ref[slot], v_ref[pl.ds(vo, tk), :])

    def emit(qi):
        acc = acc_ref[...]
        denom = jnp.where(odd, acc[:, 0:1], acc[:, MLA_V:MLA_V + 1])
        o_ref[pl.ds(pl.multiple_of(qi * tq, tq), tq), :] = (acc / denom).astype(BF16)

    def softmax(slot, restart):
        for r in range(tq // strip):
            rows = slice(r * strip, (r + 1) * strip)
            blks = [s_ref[slot, rows, c * LANES:(c + 1) * LANES] for c in range(ncol)]
            mx = blks[0]
            for c in range(1, ncol):
                mx = jnp.maximum(mx, blks[c])
            m_old = m_ref[rows, :]
            if restart is not None:
                m_old = jnp.where(restart, -jnp.inf, m_old)
            m_new = jnp.maximum(m_old, jnp.max(mx, axis=1, keepdims=True))
            m_ref[rows, :] = m_new
            a_ref[rows, :] = jnp.exp2(m_old - m_new)
            for c in range(ncol):
                p_ref[slot, rows, c * LANES:(c + 1) * LANES] = jnp.exp2(blks[c] - m_new).astype(BF16)

    m_ref[...] = jnp.zeros(m_ref.shape, F32)
    a_ref[...] = jnp.ones(a_ref.shape, F32)
    acc_ref[...] = jnp.ones(acc_ref.shape, F32)
    p_ref[1] = jnp.zeros(p_ref.shape[1:], BF16)
    scores(0, 0, 0)

    def body(j, carry):
        qi = j // ipq
        jj = j % ipq
        first = jj == 0
        for u in range(unroll):
            slot = u % 2
            t = jj * unroll + u
            if u == 0:
                accumulate(jnp.where(first, nk - 1, t - 1), 1 - slot)
                emit(jnp.where(first, jnp.maximum(qi - 1, 0), qi))
            else:
                accumulate(t - 1, 1 - slot)
            if u == unroll - 1:
                last = jj == ipq - 1
                scores(jnp.where(last, jnp.minimum(qi + 1, nq - 1), qi), jnp.where(last, 0, t + 1), 1 - slot)
            else:
                scores(qi, t + 1, 1 - slot)
            softmax(slot, first if u == 0 else None)
        return carry

    lax.fori_loop(0, nq * ipq, body, 0)
    accumulate(nk - 1, 1 - unroll % 2)
    emit(nq - 1)


def _attention(q, k, v, tq, tk, strip, unroll):
    b, h, s, w = q.shape
    assert unroll % 2 == 0 and (s // tk) % unroll == 0 and s % tq == 0 and tq % strip == 0
    spec = pl.BlockSpec((None, None, s, w), lambda bi, hi: (bi, hi, 0, 0))
    return pl.pallas_call(
        functools.partial(_attn_kernel, tq=tq, tk=tk, strip=strip, unroll=unroll),
        grid=(b, h),
        in_specs=[spec, spec, spec],
        out_specs=spec,
        out_shape=jax.ShapeDtypeStruct((b, h, s, w), BF16),
        scratch_shapes=[
            pltpu.VMEM((2, tq, tk), F32),
            pltpu.VMEM((2, tq, tk), BF16),
            pltpu.VMEM((tq, LANES), F32),
            pltpu.VMEM((tq, LANES), F32),
            pltpu.VMEM((tq, w), F32),
        ],
        compiler_params=pltpu.CompilerParams(
            dimension_semantics=("parallel", "parallel"), vmem_limit_bytes=VMEM_LIMIT),
        name="mla_attention",
    )(q, k, v)


def _odd_out_kernel(o_ref, gc_ref, um_ref, up_ref, un_ref, gd_ref, x_ref, pw_ref, ps_ref, wc_ref, wd_ref,
                    lg_ref, lb_ref, out_ref, ext_ref, *, tiles_per_seq, seq):
    tm = x_ref.shape[0]
    i = pl.program_id(0) % tiles_per_seq
    lane = lax.broadcasted_iota(jnp.int32, (tm, LANES), 1)
    low = lane < MLA_V
    gc = _silu(gc_ref[...].astype(F32))
    acc = None
    for j in range(MLA_HEADS // 2):
        pair = jnp.where(low, o_ref[2 * j], o_ref[2 * j + 1]).astype(F32)
        yc = (pair * gc[:, j * LANES:(j + 1) * LANES]).astype(BF16)
        t = _dot(yc, wc_ref[j * LANES:(j + 1) * LANES, :])
        acc = t if acc is None else acc + t

    um = um_ref[...].astype(F32)
    ext_ref[HALO:HALO + tm, :] = um
    ext_ref[0:HALO, :] = jnp.where(i > 0, up_ref[...].astype(F32), 0.0)
    ext_ref[HALO + tm:2 * HALO + tm, :] = jnp.where(i < tiles_per_seq - 1, un_ref[...].astype(F32), 0.0)
    pos = i * tm + lax.broadcasted_iota(jnp.int32, (tm, LANES), 0)
    gd = _silu(gd_ref[...].astype(F32))
    for gi, w in enumerate(POOL_WINDOWS):
        cs = slice(gi * POOL_GROUP, (gi + 1) * POOL_GROUP)
        wsum = None
        for d in range(-(w // 2), w - w // 2):
            term = ext_ref[HALO + d:HALO + d + tm, cs]
            wsum = term if wsum is None else wsum + term
        cnt = (jnp.minimum(pos + (w - w // 2), seq) - jnp.maximum(pos - w // 2, 0)).astype(F32)
        pooled = wsum / cnt - um[:, cs]
        yd = _dot(pooled.astype(BF16), pw_ref[gi]) * ps_ref[:, cs] * gd[:, cs]
        acc = acc + _dot(yd.astype(BF16), wd_ref[cs, :])
    out_ref[...] = _layer_norm_rows(ALPHA * x_ref[...] + acc, lg_ref[...], lb_ref[...])


def _odd_out(o, rest, x2d, pool_w, pool_scale, w_c, w_d, ln_g, ln_b, s, tm):
    t = x2d.shape[0]
    tps = s // tm
    hb = tm // HALO
    nh = t // HALO
    const = lambda i: (0, 0)
    pw = POOL_WIDTH
    return pl.pallas_call(
        functools.partial(_odd_out_kernel, tiles_per_seq=tps, seq=s),
        grid=(t // tm,),
        in_specs=[
            pl.BlockSpec((None, MLA_HEADS, tm, _HEAD_W), lambda i: (i // tps, 0, i % tps, 0)),
            pl.BlockSpec((tm, pw), lambda i: (i, 0)),
            pl.BlockSpec((tm, pw), lambda i: (i, 1)),
            pl.BlockSpec((HALO, pw), lambda i: (jnp.maximum(i * hb - 1, 0), 1)),
            pl.BlockSpec((HALO, pw), lambda i: (jnp.minimum((i + 1) * hb, nh - 1), 1)),
            pl.BlockSpec((tm, pw), lambda i: (i, 2)),
            pl.BlockSpec((tm, D_MODEL), lambda i: (i, 0)),
            pl.BlockSpec((len(POOL_WINDOWS), POOL_GROUP, POOL_GROUP), lambda i: (0, 0, 0)),
            pl.BlockSpec((1, pw), const),
            pl.BlockSpec((MLA_WIDTH, D_MODEL), const),
            pl.BlockSpec((pw, D_MODEL), const),
            pl.BlockSpec((1, D_MODEL), const),
            pl.BlockSpec((1, D_MODEL), const),
        ],
        out_specs=pl.BlockSpec((tm, D_MODEL), lambda i: (i, 0)),
        out_shape=jax.ShapeDtypeStruct((t, D_MODEL), F32),
        scratch_shapes=[pltpu.VMEM((tm + 2 * HALO, pw), F32)],
        compiler_params=pltpu.CompilerParams(
            dimension_semantics=("parallel",), vmem_limit_bytes=VMEM_LIMIT),
        name="odd_out_proj",
    )(o, rest, rest, rest, rest, rest, x2d, pool_w, pool_scale, w_c, w_d, ln_g, ln_b)


def _rep3(v):
    z = jnp.zeros((LANES - 3 * SSD_HEADS,), F32)
    return jnp.concatenate([v, v, v, z])[None, :]


def _even_params(w_in, conv_w, conv_b, a_log, dt_bias, d_skip, norm_g, sc_conv_w, w_out):
    o_z = 0
    o_xbc = SSD_INNER
    o_dt = o_xbc + SSD_XBC
    o_sc = o_dt + 2 * SSD_HEADS
    w_main = jnp.concatenate([w_in[:, o_xbc:o_dt], w_in[:, o_z:o_xbc], w_in[:, o_sc:]], axis=1).astype(BF16)
    zpad = jnp.zeros((D_MODEL, LANES - 3 * SSD_HEADS), F32)
    wdt = []
    for k in range(2):
        wk = w_in[:, o_dt + k * SSD_HEADS:o_dt + (k + 1) * SSD_HEADS]
        wdt.append(jnp.concatenate([wk, wk, wk, zpad], axis=1))
    wdt = jnp.concatenate(wdt, axis=1)
    wdt_hi = wdt.astype(BF16)
    wdt_lo = (wdt - wdt_hi.astype(F32)).astype(BF16)
    a_rows = [_rep3(-jnp.exp(a_log[k].astype(F32)) * LOG2E) for k in range(2)]
    dtb_rows = [_rep3(dt_bias[k].astype(F32)) for k in range(2)]
    dsk_rows = [jnp.repeat(d_skip[k].astype(F32), SSD_HEAD_DIM)[None, :] for k in range(2)]
    src = jnp.arange(LANES)[:, None]
    dst = jnp.arange(SSD_INNER)[None, :] // SSD_HEAD_DIM
    e_mat = jnp.where((src < 2 * SSD_HEADS) & (src % SSD_HEADS == dst), 1.0, 0.0).astype(BF16)
    trow = jnp.arange(SSD_CHUNK)[:, None]
    wcol = jnp.arange(SSD_CHUNK + 2 * HALO_B)[None, :]
    shifts = jnp.stack([jnp.where(wcol == trow + HALO_B + (k - SSD_CONV_LEFT), 1.0, 0.0)
                        for k in range(SSD_CONV) if k != SSD_CONV_LEFT]).astype(BF16)
    return dict(shifts=shifts,
        w_main=w_main, wdt_hi=wdt_hi, wdt_lo=wdt_lo, conv_w=conv_w.astype(F32), conv_b=conv_b[None, :].astype(F32),
        a_rows=a_rows, dtb_rows=dtb_rows, dsk_rows=dsk_rows, e_mat=e_mat, norm_g=norm_g[None, :].astype(F32),
        sc_w=sc_conv_w.astype(F32), w_a=w_out[:SSD_INNER].astype(BF16), w_b=w_out[SSD_INNER:].astype(BF16))


def _odd_params(w_in, q_norm_g, w_uq, kv_norm_g, w_ukv, pool_w, pool_scale, w_out):
    o_ckv = MLA_Q_RANK
    o_kr = o_ckv + MLA_KV_RANK
    o_rest = o_kr + MLA_ROPE
    half = MLA_ROPE // 2
    z64 = jnp.zeros((D_MODEL, MLA_NOPE), F32)
    z32 = jnp.zeros((D_MODEL, LANES - MLA_NOPE - MLA_ROPE), F32)
    kr1 = w_in[:, o_kr:o_kr + half]
    kr2 = w_in[:, o_kr + half:o_kr + MLA_ROPE]
    w2 = jnp.concatenate([w_in[:, :o_kr], z64, kr1, kr2, z32, z64, kr2, kr1, z32, w_in[:, o_rest:]],
                         axis=1).astype(BF16)
    qd = MLA_NOPE + MLA_ROPE
    w3 = w_uq.reshape(MLA_Q_RANK, MLA_HEADS, qd)
    zq = lambda n: jnp.zeros((MLA_Q_RANK, MLA_HEADS, n), F32)
    wq = jnp.concatenate([w3, zq(_HEAD_W - qd)], axis=-1).reshape(MLA_Q_RANK, -1).astype(BF16)
    wqs = jnp.concatenate([zq(MLA_NOPE), w3[..., MLA_NOPE + half:], w3[..., MLA_NOPE:MLA_NOPE + half],
                           zq(_HEAD_W - qd)], axis=-1).reshape(MLA_Q_RANK, -1).astype(BF16)
    w4 = w_ukv.reshape(MLA_KV_RANK, MLA_HEADS, MLA_NOPE + MLA_V)
    zk = jnp.zeros((MLA_KV_RANK, MLA_HEADS, _HEAD_W - MLA_NOPE), F32)
    wkn = jnp.concatenate([w4[..., :MLA_NOPE], zk], axis=-1).reshape(MLA_KV_RANK, -1).astype(BF16)
    zv = jnp.zeros((MLA_KV_RANK, _HEAD_W - MLA_V), F32)
    wv_blocks = []
    vone = []
    for h in range(MLA_HEADS):
        vh = w4[:, h, MLA_NOPE:]
        wv_blocks.append(jnp.concatenate([vh, zv] if h % 2 == 0 else [zv, vh], axis=-1))
        vone.append(jnp.zeros((_HEAD_W,), F32).at[MLA_V if h % 2 == 0 else 0].set(1.0))
    wv = jnp.concatenate(wv_blocks, axis=-1).astype(BF16)
    vone = jnp.concatenate(vone)[None, :]
    inv_freq = ROPE_THETA ** (-jnp.arange(half, dtype=F32) / half)
    frq = inv_freq[:, None]
    src = jnp.arange(LANES)[:, None]
    lane = jnp.arange(LANES)[None, :]
    rot = (lane >= MLA_NOPE) & (lane < qd) & (src < 2 * half) & ((lane - MLA_NOPE) % half == src % half)
    place_cos = jnp.where(rot | ((lane < MLA_NOPE) & (src == 2 * half)), 1.0, 0.0)
    place_sin = jnp.where(rot, jnp.where(lane < MLA_NOPE + half, -1.0, 1.0), 0.0)
    sgn = jnp.stack([place_cos, place_sin]).astype(BF16)
    return dict(
        w2=w2, qg=q_norm_g[None, :].astype(F32), kg=kv_norm_g[None, :].astype(F32), wq=wq, wqs=wqs, wkn=wkn,
        wv=wv, vone=vone, frq=frq, sgn=sgn, pool_w=pool_w.astype(BF16), pool_scale=pool_scale[None, :].astype(F32),
        w_c=w_out[:MLA_WIDTH].astype(BF16), w_d=w_out[MLA_WIDTH:].astype(BF16))


def _pick(n, pref):
    t = min(pref, n)
    while n % t:
        t //= 2
    return t


def _even_layer(x2d, s, p, ln_g, ln_b):
    t = x2d.shape[0]
    proj, dt_raw = _even_in_proj(x2d, p["w_main"], p["wdt_hi"], p["wdt_lo"], _pick(t, 1024), 3584)
    yb, xact = _ssd_backward_sweep(proj, dt_raw, p["shifts"], p["conv_w"], p["conv_b"], p["a_rows"][1],
                                   p["dtb_rows"][1], p["dsk_rows"][1], p["e_mat"], s)
    ya = _ssd_forward_sweep(xact, dt_raw, proj, yb, p["a_rows"][0], p["dtb_rows"][0], p["dsk_rows"][0],
                            p["e_mat"], p["norm_g"], s)
    return _even_out(ya, proj, x2d, p["sc_w"], p["w_a"], p["w_b"], ln_g[None, :], ln_b[None, :], s, _pick(s, 512))


def _odd_layer(x2d, pos2d, s, p, ln_g, ln_b):
    tm = _pick(s, 512)
    q, k, v, rest = _odd_in(x2d, pos2d.reshape(-1, 1, tm), p["w2"], p["qg"], p["kg"], p["wq"], p["wqs"], p["wkn"], p["wv"],
                            p["frq"], p["sgn"], p["vone"], s, tm)
    o = _attention(q, k, v, _pick(s, 512), _pick(s // 32, 512), 64, 32)
    return _odd_out(o, rest, x2d, p["pool_w"], p["pool_scale"], p["w_c"], p["w_d"], ln_g[None, :], ln_b[None, :],
                    s, tm)


def kernel(x, positions, ev_w_in, ev_conv_w, ev_conv_b, ev_a_log, ev_dt_bias, ev_d_skip, ev_norm_g, ev_sc_conv_w,
           ev_w_out, ev_ln_g, ev_ln_b, od_w_in, od_q_norm_g, od_w_uq, od_kv_norm_g, od_w_ukv, od_pool_w,
           od_pool_scale, od_w_out, od_ln_g, od_ln_b):
    b, s, d = x.shape
    x2d = x.reshape(b * s, d)
    pos2d = positions.reshape(b * s, 1)
    for layer in range(DEPTH):
        i = layer // 2
        if layer % 2 == 0:
            p = _even_params(ev_w_in[i], ev_conv_w[i], ev_conv_b[i], ev_a_log[i], ev_dt_bias[i], ev_d_skip[i],
                             ev_norm_g[i], ev_sc_conv_w[i], ev_w_out[i])
            x2d = _even_layer(x2d, s, p, ev_ln_g[i], ev_ln_b[i])
        else:
            p = _odd_params(od_w_in[i], od_q_norm_g[i], od_w_uq[i], od_kv_norm_g[i], od_w_ukv[i], od_pool_w[i],
                            od_pool_scale[i], od_w_out[i])
            x2d = _odd_layer(x2d, pos2d, s, p, od_ln_g[i], od_ln_b[i])
    return x2d.reshape(b, s, d)
```

```python
import functools
import math

import jax
import jax.numpy as jnp
from jax import lax
from jax.experimental import pallas as pl
from jax.experimental.pallas import tpu as pltpu

F32 = jnp.float32
BF16 = jnp.bfloat16

D_MODEL = 1024
DEPTH = 2
SSD_HEADS = 16
SSD_HEAD_DIM = 64
SSD_INNER = SSD_HEADS * SSD_HEAD_DIM
SSD_GROUPS = 4
SSD_STATE = 128
SSD_CONV = 4
SSD_CONV_LEFT = 2
SSD_CHUNK = 128
SSD_STEP = 8
SSD_XBC = SSD_INNER + 2 * SSD_GROUPS * SSD_STATE
SC_WIDTH = 1024
MLA_HEADS = 8
MLA_Q_RANK = 256
MLA_KV_RANK = 128
MLA_NOPE = 64
MLA_ROPE = 32
MLA_V = 64
MLA_WIDTH = MLA_HEADS * MLA_V
ROPE_THETA = 10000.0
ATTN_SCALE = (MLA_NOPE + MLA_ROPE) ** -0.5
LOG2E = math.log2(math.e)
POOL_WINDOWS = (2, 4, 8, 16)
POOL_GROUP = 128
POOL_WIDTH = POOL_GROUP * len(POOL_WINDOWS)
EPS = 1e-5
ALPHA = (2 * DEPTH) ** 0.25

LANES = 128
HALO = 8
HALO_B = 16
VMEM_LIMIT = 56 * 1024 * 1024


def _dot(a, b):
    return jnp.dot(a, b, preferred_element_type=F32)


def _silu(v):
    h = 0.5 * v
    return h + h * jnp.tanh(h)


def _softplus(v):
    return jnp.maximum(v, 0.0) + jnp.log1p(jnp.exp(-jnp.abs(v)))


def _split_bf16(v, pieces):
    out = []
    r = v
    for _ in range(pieces):
        p = r.astype(BF16)
        out.append(p)
        r = r - p.astype(F32)
    return out


def _layer_norm_rows(r, g, b):
    mu = jnp.mean(r, axis=-1, keepdims=True)
    d = r - mu
    var = jnp.mean(d * d, axis=-1, keepdims=True)
    return d * lax.rsqrt(var + EPS) * g + b


def _rms_norm_rows(v, g):
    return v * lax.rsqrt(jnp.mean(v * v, axis=-1, keepdims=True) + EPS) * g


def _even_in_proj_kernel(x_ref, w_ref, wdh_ref, wdl_ref, o_ref, dt_ref, xb_ref):
    @pl.when(pl.program_id(1) == 0)
    def _():
        x = x_ref[...]
        xh = x.astype(BF16)
        xb_ref[...] = xh
        xl = (x - xh.astype(F32)).astype(BF16)
        wh = wdh_ref[...]
        dt_ref[...] = _dot(xh, wh) + _dot(xl, wh) + _dot(xh, wdl_ref[...])

    o_ref[...] = _dot(xb_ref[...], w_ref[...]).astype(BF16)


def _even_in_proj(x2d, w_main, wdt_hi, wdt_lo, tm, tn):
    t, k = x2d.shape
    n = w_main.shape[1]
    ndt = wdt_hi.shape[1]
    return pl.pallas_call(
        _even_in_proj_kernel,
        grid=(t // tm, n // tn),
        in_specs=[
            pl.BlockSpec((tm, k), lambda i, j: (i, 0)),
            pl.BlockSpec((k, tn), lambda i, j: (0, j)),
            pl.BlockSpec((k, ndt), lambda i, j: (0, 0)),
            pl.BlockSpec((k, ndt), lambda i, j: (0, 0)),
        ],
        out_specs=[
            pl.BlockSpec((tm, tn), lambda i, j: (i, j)),
            pl.BlockSpec((tm, ndt), lambda i, j: (i, 0)),
        ],
        out_shape=[jax.ShapeDtypeStruct((t, n), BF16), jax.ShapeDtypeStruct((t, ndt), F32)],
        scratch_shapes=[pltpu.VMEM((tm, k), BF16)],
        compiler_params=pltpu.CompilerParams(
            dimension_semantics=("parallel", "arbitrary"), vmem_limit_bytes=VMEM_LIMIT),
        name="even_in_proj",
    )(x2d, w_main, wdt_hi, wdt_lo)


def _conv_silu(first, last, xm_ref, xp_ref, xn_ref, sh_ref, cw_ref, cb_ref, xact_ref):
    rows = xm_ref.shape[0]
    q = SSD_CHUNK
    blk = 2 * LANES
    taps = [k for k in range(SSD_CONV) if k != SSD_CONV_LEFT]
    for j in range(SSD_XBC // blk):
        sl = slice(j * blk, (j + 1) * blk)
        xp = xp_ref[:, sl]
        xn = xn_ref[:, sl]
        ext = jnp.concatenate([jnp.where(first, jnp.zeros_like(xp), xp), xm_ref[:, sl],
                               jnp.where(last, jnp.zeros_like(xn), xn)], axis=0)
        for r0 in range(0, rows, q):
            win = ext[r0:r0 + q + 2 * HALO_B]
            acc = cb_ref[:, sl] + cw_ref[SSD_CONV_LEFT:SSD_CONV_LEFT + 1, sl] * xm_ref[r0:r0 + q, sl].astype(F32)
            for i, k in enumerate(taps):
                acc = acc + cw_ref[k:k + 1, sl] * _dot(sh_ref[i], win)
            xact_ref[r0:r0 + q, sl] = _silu(acc).astype(BF16)


def _ssd_chunk(reverse, r0, xact_ref, dt_ref, a_ref, dtb_ref, dsk_ref, e_ref, h_ref, emit):
    q = SSD_CHUNK
    rs = slice(r0, r0 + q)

    dt = _softplus(dt_ref[rs, :] + dtb_ref[...])
    da = dt * a_ref[...]
    row = lax.broadcasted_iota(jnp.int32, (q, q), 0)
    col = lax.broadcasted_iota(jnp.int32, (q, q), 1)
    keep = (col >= row) if reverse else (col <= row)
    tri = jnp.where(keep, 1.0, 0.0).astype(BF16)
    p1, p2, p3 = _split_bf16(da, 3)
    cs = _dot(tri, p1) + _dot(tri, p2) + _dot(tri, p3)
    cs_t = cs.T
    end = 0 if reverse else q - 1
    ecs = jnp.exp2(cs)
    dec = jnp.exp2(cs[end:end + 1, :] - cs)

    lane = lax.broadcasted_iota(jnp.int32, (q, LANES), 1)

    def pieces(v):
        hi, lo = _split_bf16(v, 2)
        return jnp.where(lane < SSD_HEADS, hi, lo)

    dt_p, ecs_p, dec_p = pieces(dt), pieces(ecs), pieces(dec)
    half = lane < SSD_HEAD_DIM
    gw = (SSD_HEADS // SSD_GROUPS) * SSD_HEAD_DIM
    o_b = SSD_INNER
    o_c = SSD_INNER + SSD_GROUPS * SSD_STATE

    for g in range(SSD_GROUPS):
        gs = slice(g * gw, (g + 1) * gw)
        e_g = e_ref[:, gs]
        dt_g = _dot(dt_p, e_g)
        ecs_g = _dot(ecs_p, e_g)
        dec_g = _dot(dec_p, e_g)
        xs_g = xact_ref[rs, gs].astype(F32)
        xd_g = xs_g * dt_g
        xd_b = xd_g.astype(BF16)
        xdd_b = (xd_g * dec_g).astype(BF16)
        cg_b = xact_ref[rs, o_c + g * SSD_STATE:o_c + (g + 1) * SSD_STATE]
        bgt_b = xact_ref[rs, o_b + g * SSD_STATE:o_b + (g + 1) * SSD_STATE].astype(F32).T.astype(BF16)
        cbm = _dot(cg_b, bgt_b)
        h_in = h_ref[g]
        y_off = _dot(cg_b, h_in.astype(BF16))
        st = _dot(bgt_b, xdd_b)
        h_ref[g] = h_in * ecs_g[end:end + 1, :] + st
        for jj in range(gw // LANES):
            js = slice(jj * LANES, (jj + 1) * LANES)
            cidx = g * gw + jj * LANES
            xd_blk = xd_b[:, js]
            ms, rhs = [], []
            for hh in range(2):
                h = (cidx // SSD_HEAD_DIM) + hh
                diff = cs[:, h:h + 1] - cs_t[h:h + 1, :]
                ms.append((cbm * jnp.where(keep, jnp.exp2(diff), 0.0)).astype(BF16))
                rhs.append(jnp.where(half if hh == 0 else jnp.logical_not(half), xd_blk, jnp.zeros_like(xd_blk)))
            yd = _dot(jnp.concatenate(ms, axis=1), jnp.concatenate(rhs, axis=0))
            y_blk = yd + y_off[:, js] * ecs_g[:, js] + dsk_ref[:, cidx:cidx + LANES] * xs_g[:, js]
            emit(r0, cidx, y_blk)


def _reset_state(h_ref):
    @pl.when(pl.program_id(1) == 0)
    def _():
        h_ref[...] = jnp.zeros_like(h_ref)


def _ssd_bwd_kernel(xm_ref, xp_ref, xn_ref, sh_ref, dt_ref, cw_ref, cb_ref, a_ref, dtb_ref, dsk_ref, e_ref,
                    y_ref, xact_ref, h_ref):
    nt = pl.num_programs(1)
    tile = nt - 1 - pl.program_id(1)
    _conv_silu(tile == 0, tile == nt - 1, xm_ref, xp_ref, xn_ref, sh_ref, cw_ref, cb_ref, xact_ref)
    _reset_state(h_ref)

    def emit(r0, cidx, y_blk):
        y_ref[r0:r0 + SSD_CHUNK, cidx:cidx + LANES] = y_blk

    for sub in reversed(range(SSD_STEP)):
        _ssd_chunk(True, sub * SSD_CHUNK, xact_ref, dt_ref, a_ref, dtb_ref, dsk_ref, e_ref, h_ref, emit)


def _ssd_fwd_kernel(xact_ref, dt_ref, a_ref, dtb_ref, dsk_ref, e_ref, z_ref, yb_ref, ng_ref,
                    y_ref, h_ref, yz_ref):
    _reset_state(h_ref)

    def emit(r0, cidx, y_blk):
        rs = slice(r0, r0 + SSD_CHUNK)
        z = z_ref[rs, cidx:cidx + LANES].astype(F32)
        yz_ref[rs, cidx:cidx + LANES] = (y_blk + yb_ref[rs, cidx:cidx + LANES]) * _silu(z)

    for sub in range(SSD_STEP):
        _ssd_chunk(False, sub * SSD_CHUNK, xact_ref, dt_ref, a_ref, dtb_ref, dsk_ref, e_ref, h_ref, emit)
    y_ref[...] = _rms_norm_rows(yz_ref[...], ng_ref[...]).astype(BF16)


def _ssd_backward_sweep(proj, dt_raw, shifts, conv_w, conv_b, a_row, dtb_row, dsk_row, e_mat, s):
    t = proj.shape[0]
    rows = SSD_STEP * SSD_CHUNK
    nt = s // rows
    hb = rows // HALO_B
    nhb = s // HALO_B
    tl = lambda c: nt - 1 - c
    main = lambda bi, c: (bi * nt + tl(c), 0)
    prev = lambda bi, c: (bi * nhb + jnp.maximum(tl(c) * hb - 1, 0), 0)
    nxt = lambda bi, c: (bi * nhb + jnp.minimum((tl(c) + 1) * hb, nhb - 1), 0)
    const = lambda bi, c: (0, 0)
    return pl.pallas_call(
        _ssd_bwd_kernel,
        grid=(t // s, nt),
        in_specs=[
            pl.BlockSpec((rows, SSD_XBC), main),
            pl.BlockSpec((HALO_B, SSD_XBC), prev),
            pl.BlockSpec((HALO_B, SSD_XBC), nxt),
            pl.BlockSpec(shifts.shape, lambda bi, c: (0, 0, 0)),
            pl.BlockSpec((rows, LANES), lambda bi, c: (bi * nt + tl(c), 1)),
            pl.BlockSpec((SSD_CONV, SSD_XBC), const),
            pl.BlockSpec((1, SSD_XBC), const),
            pl.BlockSpec((1, LANES), const),
            pl.BlockSpec((1, LANES), const),
            pl.BlockSpec((1, SSD_INNER), const),
            pl.BlockSpec((LANES, SSD_INNER), const),
        ],
        out_specs=[pl.BlockSpec((rows, SSD_INNER), main), pl.BlockSpec((rows, SSD_XBC), main)],
        out_shape=[jax.ShapeDtypeStruct((t, SSD_INNER), F32), jax.ShapeDtypeStruct((t, SSD_XBC), BF16)],
        scratch_shapes=[pltpu.VMEM((SSD_GROUPS, SSD_STATE, SSD_INNER // SSD_GROUPS), F32)],
        compiler_params=pltpu.CompilerParams(
            dimension_semantics=("parallel", "arbitrary"), vmem_limit_bytes=VMEM_LIMIT),
        name="ssd_backward_sweep",
    )(proj, proj, proj, shifts, dt_raw, conv_w, conv_b, a_row, dtb_row, dsk_row, e_mat)


def _ssd_forward_sweep(xact, dt_raw, proj, yb, a_row, dtb_row, dsk_row, e_mat, norm_g, s):
    t = xact.shape[0]
    rows = SSD_STEP * SSD_CHUNK
    nt = s // rows
    main = lambda bi, c: (bi * nt + c, 0)
    const = lambda bi, c: (0, 0)
    return pl.pallas_call(
        _ssd_fwd_kernel,
        grid=(t // s, nt),
        in_specs=[
            pl.BlockSpec((rows, SSD_XBC), main),
            pl.BlockSpec((rows, LANES), main),
            pl.BlockSpec((1, LANES), const),
            pl.BlockSpec((1, LANES), const),
            pl.BlockSpec((1, SSD_INNER), const),
            pl.BlockSpec((LANES, SSD_INNER), const),
            pl.BlockSpec((rows, SSD_INNER), lambda bi, c: (bi * nt + c, 2)),
            pl.BlockSpec((rows, SSD_INNER), main),
            pl.BlockSpec((1, SSD_INNER), const),
        ],
        out_specs=pl.BlockSpec((rows, SSD_INNER), main),
        out_shape=jax.ShapeDtypeStruct((t, SSD_INNER), BF16),
        scratch_shapes=[pltpu.VMEM((SSD_GROUPS, SSD_STATE, SSD_INNER // SSD_GROUPS), F32),
                        pltpu.VMEM((rows, SSD_INNER), F32)],
        compiler_params=pltpu.CompilerParams(
            dimension_semantics=("parallel", "arbitrary"), vmem_limit_bytes=VMEM_LIMIT),
        name="ssd_forward_sweep",
    )(xact, dt_raw, a_row, dtb_row, dsk_row, e_mat, proj, yb, norm_g)


def _even_out_kernel(ya_ref, bg_ref, cgm_ref, cgp_ref, cgn_ref, hm_ref, hp_ref, hn_ref, gt_ref, x_ref,
                     scw_ref, wa_ref, wb_ref, lg_ref, lb_ref, o_ref, yb_ref, *, tiles_per_seq):
    tm = ya_ref.shape[0]
    i = pl.program_id(0) % tiles_per_seq
    blk = 2 * LANES
    row = lax.broadcasted_iota(jnp.int32, (tm, blk), 0)
    for j in range(SC_WIDTH // blk):
        sl = slice(j * blk, (j + 1) * blk)
        ch = cgm_ref[:, sl].astype(F32) * hm_ref[:, sl].astype(F32)
        before = jnp.where(i > 0, cgp_ref[HALO - 1:HALO, sl].astype(F32) * hp_ref[HALO - 1:HALO, sl].astype(F32), 0.0)
        after = jnp.where(i < tiles_per_seq - 1, cgn_ref[0:1, sl].astype(F32) * hn_ref[0:1, sl].astype(F32), 0.0)
        up = jnp.where(row == 0, before, pltpu.roll(ch, 1, axis=0))
        dn = jnp.where(row == tm - 1, after, pltpu.roll(ch, tm - 1, axis=0))
        conv = scw_ref[0:1, sl] * up + scw_ref[1:2, sl] * ch + scw_ref[2:3, sl] * dn
        yb_ref[:, sl] = (bg_ref[:, sl].astype(F32) * conv * _silu(gt_ref[:, sl].astype(F32))).astype(BF16)
    h = _dot(ya_ref[...], wa_ref[...]) + _dot(yb_ref[...], wb_ref[...])
    o_ref[...] = _layer_norm_rows(ALPHA * x_ref[...] + h, lg_ref[...], lb_ref[...])


def _even_out(ya, proj, x2d, sc_w, w_a, w_b, ln_g, ln_b, s, tm):
    t = x2d.shape[0]
    tps = s // tm
    hb = tm // HALO
    nh = t // HALO

    def colblk(j):
        return lambda i: (i, j)

    def prev(j):
        return lambda i: (jnp.maximum(i * hb - 1, 0), j)

    def nxt(j):
        return lambda i: (jnp.minimum((i + 1) * hb, nh - 1), j)

    w = SC_WIDTH
    const = lambda i: (0, 0)
    return pl.pallas_call(
        functools.partial(_even_out_kernel, tiles_per_seq=tps),
        grid=(t // tm,),
        in_specs=[
            pl.BlockSpec((tm, w), colblk(0)),
            pl.BlockSpec((tm, w), colblk(3)),
            pl.BlockSpec((tm, w), colblk(4)), pl.BlockSpec((HALO, w), prev(4)), pl.BlockSpec((HALO, w), nxt(4)),
            pl.BlockSpec((tm, w), colblk(5)), pl.BlockSpec((HALO, w), prev(5)), pl.BlockSpec((HALO, w), nxt(5)),
            pl.BlockSpec((tm, w), colblk(6)),
            pl.BlockSpec((tm, D_MODEL), colblk(0)),
            pl.BlockSpec((3, w), const),
            pl.BlockSpec((SSD_INNER, D_MODEL), const),
            pl.BlockSpec((w, D_MODEL), const),
            pl.BlockSpec((1, D_MODEL), const),
            pl.BlockSpec((1, D_MODEL), const),
        ],
        out_specs=pl.BlockSpec((tm, D_MODEL), colblk(0)),
        out_shape=jax.ShapeDtypeStruct((t, D_MODEL), F32),
        scratch_shapes=[pltpu.VMEM((tm, w), BF16)],
        compiler_params=pltpu.CompilerParams(
            dimension_semantics=("parallel",), vmem_limit_bytes=VMEM_LIMIT),
        name="even_out_proj",
    )(ya, proj, proj, proj, proj, proj, proj, proj, proj, x2d, sc_w, w_a, w_b, ln_g, ln_b)


_O_CQ = 0
_O_CKV = MLA_Q_RANK
_O_KRA = _O_CKV + MLA_KV_RANK
_O_KRB = _O_KRA + LANES
_O_REST = _O_KRB + LANES
_ODD_REST = MLA_WIDTH + 2 * POOL_WIDTH
_ODD_COLS = _O_REST + _ODD_REST
_HEAD_W = LANES


def _odd_in_kernel(x_ref, pos_ref, w_ref, qg_ref, kg_ref, wq_ref, wqs_ref, wkn_ref, wv_ref, frq_ref,
                   sgn_ref, vone_ref, q_ref, k_ref, v_ref, r_ref):
    proj = _dot(x_ref[...].astype(BF16), w_ref[...])
    r_ref[...] = proj[:, _O_REST:].astype(BF16)
    ang = frq_ref[...] * pos_ref[...].astype(F32)
    pad = jnp.zeros((LANES - 3 * ang.shape[0], ang.shape[1]), F32)

    def table(v, place):
        hi, lo = _split_bf16(v, 2)
        stack = jnp.concatenate([hi.astype(F32), lo.astype(F32), jnp.ones_like(v), pad], axis=0)
        return _dot(stack.T.astype(BF16), place)

    cosf = table(jnp.cos(ang), sgn_ref[0])
    sinf = table(jnp.sin(ang), sgn_ref[1])
    cqn = _rms_norm_rows(proj[:, _O_CQ:_O_CQ + MLA_Q_RANK], qg_ref[...]).astype(BF16)
    qa = _dot(cqn, wq_ref[...])
    qb = _dot(cqn, wqs_ref[...])
    ckvn = _rms_norm_rows(proj[:, _O_CKV:_O_CKV + MLA_KV_RANK], kg_ref[...]).astype(BF16)
    kn = _dot(ckvn, wkn_ref[...])
    vv = _dot(ckvn, wv_ref[...]) + vone_ref[...]
    kr = proj[:, _O_KRA:_O_KRA + LANES] * cosf + proj[:, _O_KRB:_O_KRB + LANES] * sinf
    for h in range(MLA_HEADS):
        sl = slice(h * _HEAD_W, (h + 1) * _HEAD_W)
        q_ref[h] = ((qa[:, sl] * cosf + qb[:, sl] * sinf) * (ATTN_SCALE * LOG2E)).astype(BF16)
        k_ref[h] = (kn[:, sl] + kr).astype(BF16)
        v_ref[h] = vv[:, sl].astype(BF16)


def _odd_in(x2d, pos2d, w2, qg, kg, wq, wqs, wkn, wv, frq, sgn, vone, s, tm):
    t = x2d.shape[0]
    b = t // s
    tps = s // tm
    const = lambda i: (0, 0)
    hw = MLA_HEADS * _HEAD_W
    qkv_spec = pl.BlockSpec((None, MLA_HEADS, tm, _HEAD_W), lambda i: (i // tps, 0, i % tps, 0))
    qkv_shape = jax.ShapeDtypeStruct((b, MLA_HEADS, s, _HEAD_W), BF16)
    return pl.pallas_call(
        _odd_in_kernel,
        grid=(t // tm,),
        in_specs=[
            pl.BlockSpec((tm, D_MODEL), lambda i: (i, 0)),
            pl.BlockSpec((None, 1, tm), lambda i: (i, 0, 0)),
            pl.BlockSpec((D_MODEL, _ODD_COLS), const),
            pl.BlockSpec((1, MLA_Q_RANK), const),
            pl.BlockSpec((1, MLA_KV_RANK), const),
            pl.BlockSpec((MLA_Q_RANK, hw), const),
            pl.BlockSpec((MLA_Q_RANK, hw), const),
            pl.BlockSpec((MLA_KV_RANK, hw), const),
            pl.BlockSpec((MLA_KV_RANK, hw), const),
            pl.BlockSpec(frq.shape, const),
            pl.BlockSpec(sgn.shape, lambda i: (0, 0, 0)),
            pl.BlockSpec((1, hw), const),
        ],
        out_specs=[qkv_spec, qkv_spec, qkv_spec, pl.BlockSpec((tm, _ODD_REST), lambda i: (i, 0))],
        out_shape=[qkv_shape, qkv_shape, qkv_shape, jax.ShapeDtypeStruct((t, _ODD_REST), BF16)],
        compiler_params=pltpu.CompilerParams(
            dimension_semantics=("parallel",), vmem_limit_bytes=VMEM_LIMIT),
        name="odd_in_proj_qkv",
    )(x2d, pos2d, w2, qg, kg, wq, wqs, wkn, wv, frq, sgn, vone)


def _attn_kernel(q_ref, k_ref, v_ref, o_ref, s_ref, p_ref, m_ref, a_ref, acc_ref, *, tq, tk, strip, unroll):
    seq = k_ref.shape[0]
    nq = seq // tq
    nk = seq // tk
    ipq = nk // unroll
    ncol = tk // LANES
    odd = (pl.program_id(1) % 2) == 1

    def scores(qi, kt, slot):
        qo = pl.multiple_of(qi * tq, tq)
        ko = pl.multiple_of(kt * tk, tk)
        s_ref[slot] = lax.dot_general(q_ref[pl.ds(qo, tq), :], k_ref[pl.ds(ko, tk), :],
                                      (((1,), (1,)), ((), ())), preferred_element_type=F32)

    def accumulate(kt, slot):
        vo = pl.multiple_of(kt * tk, tk)
        acc_ref[...] = acc_ref[...] * a_ref[...] + _dot(p_ref[slot], v_ref[pl.ds(vo, tk), :])

    def emit(qi):
        acc = acc_ref[...]
        denom = jnp.where(odd, acc[:, 0:1], acc[:, MLA_V:MLA_V + 1])
        o_ref[pl.ds(pl.multiple_of(qi * tq, tq), tq), :] = (acc / denom).astype(BF16)

    def softmax(slot, restart):
        for r in range(tq // strip):
            rows = slice(r * strip, (r + 1) * strip)
            blks = [s_ref[slot, rows, c * LANES:(c + 1) * LANES] for c in range(ncol)]
            mx = blks[0]
            for c in range(1, ncol):
                mx = jnp.maximum(mx, blks[c])
            m_old = m_ref[rows, :]
            if restart is not None:
                m_old = jnp.where(restart, -jnp.inf, m_old)
            m_new = jnp.maximum(m_old, jnp.max(mx, axis=1, keepdims=True))
            m_ref[rows, :] = m_new
            a_ref[rows, :] = jnp.exp2(m_old - m_new)
            for c in range(ncol):
                p_ref[slot, rows, c * LANES:(c + 1) * LANES] = jnp.exp2(blks[c] - m_new).astype(BF16)

    m_ref[...] = jnp.zeros(m_ref.shape, F32)
    a_ref[...] = jnp.ones(a_ref.shape, F32)
    acc_ref[...] = jnp.ones(acc_ref.shape, F32)
    p_ref[1] = jnp.zeros(p_ref.shape[1:], BF16)
    scores(0, 0, 0)

    def body(j, carry):
        qi = j // ipq
        jj = j % ipq
        first = jj == 0
        for u in range(unroll):
            slot = u % 2
            t = jj * unroll + u
            if u == 0:
                accumulate(jnp.where(first, nk - 1, t - 1), 1 - slot)
                emit(jnp.where(first, jnp.maximum(qi - 1, 0), qi))
            else:
                accumulate(t - 1, 1 - slot)
            if u == unroll - 1:
                last = jj == ipq - 1
                scores(jnp.where(last, jnp.minimum(qi + 1, nq - 1), qi), jnp.where(last, 0, t + 1), 1 - slot)
            else:
                scores(qi, t + 1, 1 - slot)
            softmax(slot, first if u == 0 else None)
        return carry

    lax.fori_loop(0, nq * ipq, body, 0)
    accumulate(nk - 1, 1 - unroll % 2)
    emit(nq - 1)


def _attention(q, k, v, tq, tk, strip, unroll):
    b, h, s, w = q.shape
    assert unroll % 2 == 0 and (s // tk) % unroll == 0 and s % tq == 0 and tq % strip == 0
    spec = pl.BlockSpec((None, None, s, w), lambda bi, hi: (bi, hi, 0, 0))
    return pl.pallas_call(
        functools.partial(_attn_kernel, tq=tq, tk=tk, strip=strip, unroll=unroll),
        grid=(b, h),
        in_specs=[spec, spec, spec],
        out_specs=spec,
        out_shape=jax.ShapeDtypeStruct((b, h, s, w), BF16),
        scratch_shapes=[
            pltpu.VMEM((2, tq, tk), F32),
            pltpu.VMEM((2, tq, tk), BF16),
            pltpu.VMEM((tq, LANES), F32),
            pltpu.VMEM((tq, LANES), F32),
            pltpu.VMEM((tq, w), F32),
        ],
        compiler_params=pltpu.CompilerParams(
            dimension_semantics=("parallel", "parallel"), vmem_limit_bytes=VMEM_LIMIT),
        name="mla_attention",
    )(q, k, v)


def _odd_out_kernel(o_ref, gc_ref, um_ref, up_ref, un_ref, gd_ref, x_ref, pw_ref, ps_ref, wc_ref, wd_ref,
                    lg_ref, lb_ref, out_ref, *, tiles_per_seq, seq):
    tm = x_ref.shape[0]
    i = pl.program_id(0) % tiles_per_seq
    lane = lax.broadcasted_iota(jnp.int32, (tm, LANES), 1)
    low = lane < MLA_V
    gc = _silu(gc_ref[...].astype(F32))
    ycs = []
    for j in range(MLA_HEADS // 2):
        pair = jnp.where(low, o_ref[2 * j], o_ref[2 * j + 1]).astype(F32)
        ycs.append((pair * gc[:, j * LANES:(j + 1) * LANES]).astype(BF16))
    acc = _dot(jnp.concatenate(ycs, axis=1), wc_ref[...])
    yds = []

    pos = i * tm + lax.broadcasted_iota(jnp.int32, (tm, LANES), 0)
    gd = _silu(gd_ref[...].astype(F32))
    for gi, w in enumerate(POOL_WINDOWS):
        cs = slice(gi * POOL_GROUP, (gi + 1) * POOL_GROUP)
        um = um_ref[:, cs].astype(F32)
        ext = jnp.concatenate([jnp.where(i > 0, up_ref[:, cs].astype(F32), 0.0), um,
                               jnp.where(i < tiles_per_seq - 1, un_ref[:, cs].astype(F32), 0.0)], axis=0)
        run, span = ext, 1
        while span < w:
            run = run + pltpu.roll(run, span, axis=0)
            span *= 2
        lead = w - w // 2 - 1
        wsum = run[HALO + lead:HALO + lead + tm]
        cnt = (jnp.minimum(pos + (w - w // 2), seq) - jnp.maximum(pos - w // 2, 0)).astype(F32)
        pooled = wsum / cnt - um
        yds.append((_dot(pooled.astype(BF16), pw_ref[gi]) * ps_ref[:, cs] * gd[:, cs]).astype(BF16))
    acc = acc + _dot(jnp.concatenate(yds, axis=1), wd_ref[...])
    out_ref[...] = _layer_norm_rows(ALPHA * x_ref[...] + acc, lg_ref[...], lb_ref[...])


def _odd_out(o, rest, x2d, pool_w, pool_scale, w_c, w_d, ln_g, ln_b, s, tm):
    t = x2d.shape[0]
    tps = s // tm
    hb = tm // HALO
    nh = t // HALO
    const = lambda i: (0, 0)
    pw = POOL_WIDTH
    return pl.pallas_call(
        functools.partial(_odd_out_kernel, tiles_per_seq=tps, seq=s),
        grid=(t // tm,),
        in_specs=[
            pl.BlockSpec((None, MLA_HEADS, tm, _HEAD_W), lambda i: (i // tps, 0, i % tps, 0)),
            pl.BlockSpec((tm, pw), lambda i: (i, 0)),
            pl.BlockSpec((tm, pw), lambda i: (i, 1)),
            pl.BlockSpec((HALO, pw), lambda i: (jnp.maximum(i * hb - 1, 0), 1)),
            pl.BlockSpec((HALO, pw), lambda i: (jnp.minimum((i + 1) * hb, nh - 1), 1)),
            pl.BlockSpec((tm, pw), lambda i: (i, 2)),
            pl.BlockSpec((tm, D_MODEL), lambda i: (i, 0)),
            pl.BlockSpec((len(POOL_WINDOWS), POOL_GROUP, POOL_GROUP), lambda i: (0, 0, 0)),
            pl.BlockSpec((1, pw), const),
            pl.BlockSpec((MLA_WIDTH, D_MODEL), const),
            pl.BlockSpec((pw, D_MODEL), const),
            pl.BlockSpec((1, D_MODEL), const),
            pl.BlockSpec((1, D_MODEL), const),
        ],
        out_specs=pl.BlockSpec((tm, D_MODEL), lambda i: (i, 0)),
        out_shape=jax.ShapeDtypeStruct((t, D_MODEL), F32),
        compiler_params=pltpu.CompilerParams(
            dimension_semantics=("parallel",), vmem_limit_bytes=VMEM_LIMIT),
        name="odd_out_proj",
    )(o, rest, rest, rest, rest, rest, x2d, pool_w, pool_scale, w_c, w_d, ln_g, ln_b)


def _rep3(v):
    z = jnp.zeros((LANES - 3 * SSD_HEADS,), F32)
    return jnp.concatenate([v, v, v, z])[None, :]


def _even_params(w_in, conv_w, conv_b, a_log, dt_bias, d_skip, norm_g, sc_conv_w, w_out):
    o_z = 0
    o_xbc = SSD_INNER
    o_dt = o_xbc + SSD_XBC
    o_sc = o_dt + 2 * SSD_HEADS
    w_main = jnp.concatenate([w_in[:, o_xbc:o_dt], w_in[:, o_z:o_xbc], w_in[:, o_sc:]], axis=1).astype(BF16)
    zpad = jnp.zeros((D_MODEL, LANES - 3 * SSD_HEADS), F32)
    wdt = []
    for k in range(2):
        wk = w_in[:, o_dt + k * SSD_HEADS:o_dt + (k + 1) * SSD_HEADS]
        wdt.append(jnp.concatenate([wk, wk, wk, zpad], axis=1))
    wdt = jnp.concatenate(wdt, axis=1)
    wdt_hi = wdt.astype(BF16)
    wdt_lo = (wdt - wdt_hi.astype(F32)).astype(BF16)
    a_rows = [_rep3(-jnp.exp(a_log[k].astype(F32)) * LOG2E) for k in range(2)]
    dtb_rows = [_rep3(dt_bias[k].astype(F32)) for k in range(2)]
    dsk_rows = [jnp.repeat(d_skip[k].astype(F32), SSD_HEAD_DIM)[None, :] for k in range(2)]
    src = jnp.arange(LANES)[:, None]
    dst = jnp.arange(SSD_INNER)[None, :] // SSD_HEAD_DIM
    e_mat = jnp.where((src < 2 * SSD_HEADS) & (src % SSD_HEADS == dst), 1.0, 0.0).astype(BF16)
    trow = jnp.arange(SSD_CHUNK)[:, None]
    wcol = jnp.arange(SSD_CHUNK + 2 * HALO_B)[None, :]
    shifts = jnp.stack([jnp.where(wcol == trow + HALO_B + (k - SSD_CONV_LEFT), 1.0, 0.0)
                        for k in range(SSD_CONV) if k != SSD_CONV_LEFT]).astype(BF16)
    return dict(shifts=shifts,
        w_main=w_main, wdt_hi=wdt_hi, wdt_lo=wdt_lo, conv_w=conv_w.astype(F32), conv_b=conv_b[None, :].astype(F32),
        a_rows=a_rows, dtb_rows=dtb_rows, dsk_rows=dsk_rows, e_mat=e_mat, norm_g=norm_g[None, :].astype(F32),
        sc_w=sc_conv_w.astype(F32), w_a=w_out[:SSD_INNER].astype(BF16), w_b=w_out[SSD_INNER:].astype(BF16))


def _odd_params(w_in, q_norm_g, w_uq, kv_norm_g, w_ukv, pool_w, pool_scale, w_out):
    o_ckv = MLA_Q_RANK
    o_kr = o_ckv + MLA_KV_RANK
    o_rest = o_kr + MLA_ROPE
    half = MLA_ROPE // 2
    z64 = jnp.zeros((D_MODEL, MLA_NOPE), F32)
    z32 = jnp.zeros((D_MODEL, LANES - MLA_NOPE - MLA_ROPE), F32)
    kr1 = w_in[:, o_kr:o_kr + half]
    kr2 = w_in[:, o_kr + half:o_kr + MLA_ROPE]
    w2 = jnp.concatenate([w_in[:, :o_kr], z64, kr1, kr2, z32, z64, kr2, kr1, z32, w_in[:, o_rest:]],
                         axis=1).astype(BF16)
    qd = MLA_NOPE + MLA_ROPE
    w3 = w_uq.reshape(MLA_Q_RANK, MLA_HEADS, qd)
    zq = lambda n: jnp.zeros((MLA_Q_RANK, MLA_HEADS, n), F32)
    wq = jnp.concatenate([w3, zq(_HEAD_W - qd)], axis=-1).reshape(MLA_Q_RANK, -1).astype(BF16)
    wqs = jnp.concatenate([zq(MLA_NOPE), w3[..., MLA_NOPE + half:], w3[..., MLA_NOPE:MLA_NOPE + half],
                           zq(_HEAD_W - qd)], axis=-1).reshape(MLA_Q_RANK, -1).astype(BF16)
    w4 = w_ukv.reshape(MLA_KV_RANK, MLA_HEADS, MLA_NOPE + MLA_V)
    zk = jnp.zeros((MLA_KV_RANK, MLA_HEADS, _HEAD_W - MLA_NOPE), F32)
    wkn = jnp.concatenate([w4[..., :MLA_NOPE], zk], axis=-1).reshape(MLA_KV_RANK, -1).astype(BF16)
    zv = jnp.zeros((MLA_KV_RANK, _HEAD_W - MLA_V), F32)
    wv_blocks = []
    vone = []
    for h in range(MLA_HEADS):
        vh = w4[:, h, MLA_NOPE:]
        wv_blocks.append(jnp.concatenate([vh, zv] if h % 2 == 0 else [zv, vh], axis=-1))
        vone.append(jnp.zeros((_HEAD_W,), F32).at[MLA_V if h % 2 == 0 else 0].set(1.0))
    wv = jnp.concatenate(wv_blocks, axis=-1).astype(BF16)
    vone = jnp.concatenate(vone)[None, :]
    inv_freq = ROPE_THETA ** (-jnp.arange(half, dtype=F32) / half)
    frq = inv_freq[:, None]
    src = jnp.arange(LANES)[:, None]
    lane = jnp.arange(LANES)[None, :]
    rot = (lane >= MLA_NOPE) & (lane < qd) & (src < 2 * half) & ((lane - MLA_NOPE) % half == src % half)
    place_cos = jnp.where(rot | ((lane < MLA_NOPE) & (src == 2 * half)), 1.0, 0.0)
    place_sin = jnp.where(rot, jnp.where(lane < MLA_NOPE + half, -1.0, 1.0), 0.0)
    sgn = jnp.stack([place_cos, place_sin]).astype(BF16)
    return dict(
        w2=w2, qg=q_norm_g[None, :].astype(F32), kg=kv_norm_g[None, :].astype(F32), wq=wq, wqs=wqs, wkn=wkn,
        wv=wv, vone=vone, frq=frq, sgn=sgn, pool_w=pool_w.astype(BF16), pool_scale=pool_scale[None, :].astype(F32),
        w_c=w_out[:MLA_WIDTH].astype(BF16), w_d=w_out[MLA_WIDTH:].astype(BF16))


def _pick(n, pref):
    t = min(pref, n)
    while n % t:
        t //= 2
    return t


def _even_layer(x2d, s, p, ln_g, ln_b):
    t = x2d.shape[0]
    proj, dt_raw = _even_in_proj(x2d, p["w_main"], p["wdt_hi"], p["wdt_lo"], _pick(t, 1024), 3584)
    yb, xact = _ssd_backward_sweep(proj, dt_raw, p["shifts"], p["conv_w"], p["conv_b"], p["a_rows"][1],
                                   p["dtb_rows"][1], p["dsk_rows"][1], p["e_mat"], s)
    ya = _ssd_forward_sweep(xact, dt_raw, proj, yb, p["a_rows"][0], p["dtb_rows"][0], p["dsk_rows"][0],
                            p["e_mat"], p["norm_g"], s)
    return _even_out(ya, proj, x2d, p["sc_w"], p["w_a"], p["w_b"], ln_g[None, :], ln_b[None, :], s, _pick(s, 512))


def _odd_layer(x2d, pos2d, s, p, ln_g, ln_b):
    tm = _pick(s, 512)
    q, k, v, rest = _odd_in(x2d, pos2d.reshape(-1, 1, tm), p["w2"], p["qg"], p["kg"], p["wq"], p["wqs"], p["wkn"], p["wv"],
                            p["frq"], p["sgn"], p["vone"], s, tm)
    o = _attention(q, k, v, _pick(s, 512), _pick(s // 32, 512), 64, 32)
    return _odd_out(o, rest, x2d, p["pool_w"], p["pool_scale"], p["w_c"], p["w_d"], ln_g[None, :], ln_b[None, :],
                    s, tm)


def kernel(x, positions, ev_w_in, ev_conv_w, ev_conv_b, ev_a_log, ev_dt_bias, ev_d_skip, ev_norm_g, ev_sc_conv_w,
           ev_w_out, ev_ln_g, ev_ln_b, od_w_in, od_q_norm_g, od_w_uq, od_kv_norm_g, od_w_ukv, od_pool_w,
           od_pool_scale, od_w_out, od_ln_g, od_ln_b):
    b, s, d = x.shape
    x2d = x.reshape(b * s, d)
    pos2d = positions.reshape(b * s, 1)
    for layer in range(DEPTH):
        i = layer // 2
        if layer % 2 == 0:
            p = _even_params(ev_w_in[i], ev_conv_w[i], ev_conv_b[i], ev_a_log[i], ev_dt_bias[i], ev_d_skip[i],
                             ev_norm_g[i], ev_sc_conv_w[i], ev_w_out[i])
            x2d = _even_layer(x2d, s, p, ev_ln_g[i], ev_ln_b[i])
        else:
            p = _odd_params(od_w_in[i], od_q_norm_g[i], od_w_uq[i], od_kv_norm_g[i], od_w_ukv[i], od_pool_w[i],
                            od_pool_scale[i], od_w_out[i])
            x2d = _odd_layer(x2d, pos2d, s, p, od_ln_g[i], od_ln_b[i])
    return x2d.reshape(b, s, d)
```

```python
import functools
import math

import jax
import jax.numpy as jnp
from jax import lax
from jax.experimental import pallas as pl
from jax.experimental.pallas import tpu as pltpu

F32 = jnp.float32
BF16 = jnp.bfloat16

D_MODEL = 1024
DEPTH = 2
SSD_HEADS = 16
SSD_HEAD_DIM = 64
SSD_INNER = SSD_HEADS * SSD_HEAD_DIM
SSD_GROUPS = 4
SSD_STATE = 128
SSD_CONV = 4
SSD_CONV_LEFT = 2
SSD_CHUNK = 128
SSD_STEP = 8
FUSED_STEP = 4
SSD_XBC = SSD_INNER + 2 * SSD_GROUPS * SSD_STATE
SC_WIDTH = 1024
MLA_HEADS = 8
MLA_Q_RANK = 256
MLA_KV_RANK = 128
MLA_NOPE = 64
MLA_ROPE = 32
MLA_V = 64
MLA_WIDTH = MLA_HEADS * MLA_V
ROPE_THETA = 10000.0
ATTN_SCALE = (MLA_NOPE + MLA_ROPE) ** -0.5
LOG2E = math.log2(math.e)
POOL_WINDOWS = (2, 4, 8, 16)
POOL_GROUP = 128
POOL_WIDTH = POOL_GROUP * len(POOL_WINDOWS)
EPS = 1e-5
ALPHA = (2 * DEPTH) ** 0.25

LANES = 128
HALO = 8
HALO_B = 16
VMEM_LIMIT = 56 * 1024 * 1024


def _dot(a, b):
    return jnp.dot(a, b, preferred_element_type=F32)


def _silu(v):
    h = 0.5 * v
    return h + h * jnp.tanh(h)


def _softplus(v):
    return jnp.maximum(v, 0.0) + jnp.log1p(jnp.exp(-jnp.abs(v)))


def _split_bf16(v, pieces):
    out = []
    r = v
    for _ in range(pieces):
        p = r.astype(BF16)
        out.append(p)
        r = r - p.astype(F32)
    return out


def _layer_norm_rows(r, g, b):
    mu = jnp.mean(r, axis=-1, keepdims=True)
    d = r - mu
    var = jnp.mean(d * d, axis=-1, keepdims=True)
    return d * lax.rsqrt(var + EPS) * g + b


def _rms_norm_rows(v, g):
    return v * lax.rsqrt(jnp.mean(v * v, axis=-1, keepdims=True) + EPS) * g


def _even_in_proj_kernel(x_ref, w_ref, wdh_ref, wdl_ref, o_ref, dt_ref, xb_ref):
    @pl.when(pl.program_id(1) == 0)
    def _():
        x = x_ref[...]
        xh = x.astype(BF16)
        xb_ref[...] = xh
        xl = (x - xh.astype(F32)).astype(BF16)
        wh = wdh_ref[...]
        dt_ref[...] = _dot(xh, wh) + _dot(xl, wh) + _dot(xh, wdl_ref[...])

    o_ref[...] = _dot(xb_ref[...], w_ref[...]).astype(BF16)


def _even_in_proj(x2d, w_main, wdt_hi, wdt_lo, tm, tn):
    t, k = x2d.shape
    n = w_main.shape[1]
    ndt = wdt_hi.shape[1]
    return pl.pallas_call(
        _even_in_proj_kernel,
        grid=(t // tm, n // tn),
        in_specs=[
            pl.BlockSpec((tm, k), lambda i, j: (i, 0)),
            pl.BlockSpec((k, tn), lambda i, j: (0, j)),
            pl.BlockSpec((k, ndt), lambda i, j: (0, 0)),
            pl.BlockSpec((k, ndt), lambda i, j: (0, 0)),
        ],
        out_specs=[
            pl.BlockSpec((tm, tn), lambda i, j: (i, j)),
            pl.BlockSpec((tm, ndt), lambda i, j: (i, 0)),
        ],
        out_shape=[jax.ShapeDtypeStruct((t, n), BF16), jax.ShapeDtypeStruct((t, ndt), F32)],
        scratch_shapes=[pltpu.VMEM((tm, k), BF16)],
        compiler_params=pltpu.CompilerParams(
            dimension_semantics=("parallel", "arbitrary"), vmem_limit_bytes=VMEM_LIMIT),
        name="even_in_proj",
    )(x2d, w_main, wdt_hi, wdt_lo)


def _conv_silu(first, last, xm_ref, xp_ref, xn_ref, sh_ref, cw_ref, cb_ref, xact_ref):
    rows = xm_ref.shape[0]
    q = SSD_CHUNK
    blk = 2 * LANES
    taps = [k for k in range(SSD_CONV) if k != SSD_CONV_LEFT]
    for j in range(SSD_XBC // blk):
        sl = slice(j * blk, (j + 1) * blk)
        xp = xp_ref[:, sl]
        xn = xn_ref[:, sl]
        ext = jnp.concatenate([jnp.where(first, jnp.zeros_like(xp), xp), xm_ref[:, sl],
                               jnp.where(last, jnp.zeros_like(xn), xn)], axis=0)
        for r0 in range(0, rows, q):
            win = ext[r0:r0 + q + 2 * HALO_B]
            acc = cb_ref[:, sl] + cw_ref[SSD_CONV_LEFT:SSD_CONV_LEFT + 1, sl] * xm_ref[r0:r0 + q, sl].astype(F32)
            for i, k in enumerate(taps):
                acc = acc + cw_ref[k:k + 1, sl] * _dot(sh_ref[i], win)
            xact_ref[r0:r0 + q, sl] = _silu(acc).astype(BF16)


def _ssd_chunk(reverse, r0, xact_ref, dt_ref, a_ref, dtb_ref, dsk_ref, e_ref, h_ref, emit):
    q = SSD_CHUNK
    rs = slice(r0, r0 + q)

    dt = _softplus(dt_ref[rs, :] + dtb_ref[...])
    da = dt * a_ref[...]
    row = lax.broadcasted_iota(jnp.int32, (q, q), 0)
    col = lax.broadcasted_iota(jnp.int32, (q, q), 1)
    keep = (col >= row) if reverse else (col <= row)
    tri = jnp.where(keep, 1.0, 0.0).astype(BF16)
    p1, p2, p3 = _split_bf16(da, 3)
    cs = _dot(tri, p1) + _dot(tri, p2) + _dot(tri, p3)
    cs_t = cs.T
    end = 0 if reverse else q - 1
    ecs = jnp.exp2(cs)
    dec = jnp.exp2(cs[end:end + 1, :] - cs)

    lane = lax.broadcasted_iota(jnp.int32, (q, LANES), 1)

    def pieces(v):
        hi, lo = _split_bf16(v, 2)
        return jnp.where(lane < SSD_HEADS, hi, lo)

    dt_p, ecs_p, dec_p = pieces(dt), pieces(ecs), pieces(dec)
    half = lane < SSD_HEAD_DIM
    gw = (SSD_HEADS // SSD_GROUPS) * SSD_HEAD_DIM
    o_b = SSD_INNER
    o_c = SSD_INNER + SSD_GROUPS * SSD_STATE

    for g in range(SSD_GROUPS):
        gs = slice(g * gw, (g + 1) * gw)
        e_g = e_ref[:, gs]
        dt_g = _dot(dt_p, e_g)
        ecs_g = _dot(ecs_p, e_g)
        dec_g = _dot(dec_p, e_g)
        xs_g = xact_ref[rs, gs].astype(F32)
        xd_g = xs_g * dt_g
        xd_b = xd_g.astype(BF16)
        xdd_b = (xd_g * dec_g).astype(BF16)
        cg_b = xact_ref[rs, o_c + g * SSD_STATE:o_c + (g + 1) * SSD_STATE]
        bgt_b = xact_ref[rs, o_b + g * SSD_STATE:o_b + (g + 1) * SSD_STATE].astype(F32).T.astype(BF16)
        cbm = _dot(cg_b, bgt_b)
        h_in = h_ref[g]
        y_off = _dot(cg_b, h_in.astype(BF16))
        st = _dot(bgt_b, xdd_b)
        h_ref[g] = h_in * ecs_g[end:end + 1, :] + st
        for jj in range(gw // LANES):
            js = slice(jj * LANES, (jj + 1) * LANES)
            cidx = g * gw + jj * LANES
            xd_blk = xd_b[:, js]
            ms, rhs = [], []
            for hh in range(2):
                h = (cidx // SSD_HEAD_DIM) + hh
                diff = cs[:, h:h + 1] - cs_t[h:h + 1, :]
                ms.append((cbm * jnp.where(keep, jnp.exp2(diff), 0.0)).astype(BF16))
                rhs.append(jnp.where(half if hh == 0 else jnp.logical_not(half), xd_blk, jnp.zeros_like(xd_blk)))
            yd = _dot(jnp.concatenate(ms, axis=1), jnp.concatenate(rhs, axis=0))
            y_blk = yd + y_off[:, js] * ecs_g[:, js] + dsk_ref[:, cidx:cidx + LANES] * xs_g[:, js]
            emit(r0, cidx, y_blk)


def _reset_state(h_ref):
    @pl.when(pl.program_id(1) == 0)
    def _():
        h_ref[...] = jnp.zeros_like(h_ref)


def _ssd_bwd_kernel(xm_ref, xp_ref, xn_ref, sh_ref, dt_ref, cw_ref, cb_ref, a_ref, dtb_ref, dsk_ref, e_ref,
                    y_ref, xact_ref, h_ref):
    nt = pl.num_programs(1)
    tile = nt - 1 - pl.program_id(1)
    _conv_silu(tile == 0, tile == nt - 1, xm_ref, xp_ref, xn_ref, sh_ref, cw_ref, cb_ref, xact_ref)
    _reset_state(h_ref)

    def emit(r0, cidx, y_blk):
        y_ref[r0:r0 + SSD_CHUNK, cidx:cidx + LANES] = y_blk

    for sub in reversed(range(SSD_STEP)):
        _ssd_chunk(True, sub * SSD_CHUNK, xact_ref, dt_ref, a_ref, dtb_ref, dsk_ref, e_ref, h_ref, emit)


def _short_conv(i, tiles_per_seq, bg_ref, cgm_ref, cgp_ref, cgn_ref, hm_ref, hp_ref, hn_ref, gt_ref, scw_ref,
                yb_ref):
    tm = bg_ref.shape[0]
    blk = 2 * LANES
    row = lax.broadcasted_iota(jnp.int32, (tm, blk), 0)
    for j in range(SC_WIDTH // blk):
        sl = slice(j * blk, (j + 1) * blk)
        ch = cgm_ref[:, sl].astype(F32) * hm_ref[:, sl].astype(F32)
        before = jnp.where(i > 0, cgp_ref[HALO - 1:HALO, sl].astype(F32) * hp_ref[HALO - 1:HALO, sl].astype(F32), 0.0)
        after = jnp.where(i < tiles_per_seq - 1, cgn_ref[0:1, sl].astype(F32) * hn_ref[0:1, sl].astype(F32), 0.0)
        up = jnp.where(row == 0, before, pltpu.roll(ch, 1, axis=0))
        dn = jnp.where(row == tm - 1, after, pltpu.roll(ch, tm - 1, axis=0))
        conv = scw_ref[0:1, sl] * up + scw_ref[1:2, sl] * ch + scw_ref[2:3, sl] * dn
        yb_ref[:, sl] = (bg_ref[:, sl].astype(F32) * conv * _silu(gt_ref[:, sl].astype(F32))).astype(BF16)


def _ssd_fwd_out_kernel(xact_ref, dt_ref, a_ref, dtb_ref, dsk_ref, e_ref, z_ref, yb_ref, ng_ref,
                        bg_ref, cgm_ref, cgp_ref, cgn_ref, hm_ref, hp_ref, hn_ref, gt_ref, x_ref,
                        scw_ref, wa_ref, wb_ref, lg_ref, lb_ref, o_ref, h_ref, yz_ref, ya_ref, ysc_ref):
    c = pl.program_id(1)
    nt = pl.num_programs(1) - 1
    tile = jnp.minimum(c, nt - 1)

    @pl.when(c == 0)
    def _():
        h_ref[...] = jnp.zeros_like(h_ref)
        ya_ref[...] = jnp.zeros_like(ya_ref)
        ysc_ref[...] = jnp.zeros_like(ysc_ref)

    hmix = _dot(ya_ref[...], wa_ref[...]) + _dot(ysc_ref[...], wb_ref[...])
    o_ref[...] = _layer_norm_rows(ALPHA * x_ref[...] + hmix, lg_ref[...], lb_ref[...])

    def emit(r0, cidx, y_blk):
        rs = slice(r0, r0 + SSD_CHUNK)
        z = z_ref[rs, cidx:cidx + LANES].astype(F32)
        yz_ref[rs, cidx:cidx + LANES] = (y_blk + yb_ref[rs, cidx:cidx + LANES]) * _silu(z)

    for sub in range(FUSED_STEP):
        _ssd_chunk(False, sub * SSD_CHUNK, xact_ref, dt_ref, a_ref, dtb_ref, dsk_ref, e_ref, h_ref, emit)
    ya_ref[...] = _rms_norm_rows(yz_ref[...], ng_ref[...]).astype(BF16)
    _short_conv(tile, nt, bg_ref, cgm_ref, cgp_ref, cgn_ref, hm_ref, hp_ref, hn_ref, gt_ref, scw_ref, ysc_ref)


def _ssd_backward_sweep(proj, dt_raw, shifts, conv_w, conv_b, a_row, dtb_row, dsk_row, e_mat, s):
    t = proj.shape[0]
    rows = SSD_STEP * SSD_CHUNK
    nt = s // rows
    hb = rows // HALO_B
    nhb = s // HALO_B
    tl = lambda c: nt - 1 - c
    main = lambda bi, c: (bi * nt + tl(c), 0)
    prev = lambda bi, c: (bi * nhb + jnp.maximum(tl(c) * hb - 1, 0), 0)
    nxt = lambda bi, c: (bi * nhb + jnp.minimum((tl(c) + 1) * hb, nhb - 1), 0)
    const = lambda bi, c: (0, 0)
    return pl.pallas_call(
        _ssd_bwd_kernel,
        grid=(t // s, nt),
        in_specs=[
            pl.BlockSpec((rows, SSD_XBC), main),
            pl.BlockSpec((HALO_B, SSD_XBC), prev),
            pl.BlockSpec((HALO_B, SSD_XBC), nxt),
            pl.BlockSpec(shifts.shape, lambda bi, c: (0, 0, 0)),
            pl.BlockSpec((rows, LANES), lambda bi, c: (bi * nt + tl(c), 1)),
            pl.BlockSpec((SSD_CONV, SSD_XBC), const),
            pl.BlockSpec((1, SSD_XBC), const),
            pl.BlockSpec((1, LANES), const),
            pl.BlockSpec((1, LANES), const),
            pl.BlockSpec((1, SSD_INNER), const),
            pl.BlockSpec((LANES, SSD_INNER), const),
        ],
        out_specs=[pl.BlockSpec((rows, SSD_INNER), main), pl.BlockSpec((rows, SSD_XBC), main)],
        out_shape=[jax.ShapeDtypeStruct((t, SSD_INNER), F32), jax.ShapeDtypeStruct((t, SSD_XBC), BF16)],
        scratch_shapes=[pltpu.VMEM((SSD_GROUPS, SSD_STATE, SSD_INNER // SSD_GROUPS), F32)],
        compiler_params=pltpu.CompilerParams(
            dimension_semantics=("parallel", "arbitrary"), vmem_limit_bytes=VMEM_LIMIT),
        name="ssd_backward_sweep",
    )(proj, proj, proj, shifts, dt_raw, conv_w, conv_b, a_row, dtb_row, dsk_row, e_mat)


def _ssd_forward_out(xact, dt_raw, proj, yb, x2d, a_row, dtb_row, dsk_row, e_mat, norm_g, sc_w, w_a, w_b,
                     ln_g, ln_b, s):
    t = xact.shape[0]
    rows = FUSED_STEP * SSD_CHUNK
    nt = s // rows
    hb = rows // HALO
    nhb = s // HALO
    w = SC_WIDTH
    tile = lambda c: jnp.minimum(c, nt - 1)
    done = lambda c: jnp.maximum(c - 1, 0)

    def cur(j):
        return lambda bi, c: (bi * nt + tile(c), j)

    def prev(j):
        return lambda bi, c: (bi * nhb + jnp.maximum(tile(c) * hb - 1, 0), j)

    def nxt(j):
        return lambda bi, c: (bi * nhb + jnp.minimum((tile(c) + 1) * hb, nhb - 1), j)

    const = lambda bi, c: (0, 0)
    out_blk = lambda bi, c: (bi * nt + done(c), 0)
    return pl.pallas_call(
        _ssd_fwd_out_kernel,
        grid=(t // s, nt + 1),
        in_specs=[
            pl.BlockSpec((rows, SSD_XBC), cur(0)),
            pl.BlockSpec((rows, LANES), cur(0)),
            pl.BlockSpec((1, LANES), const),
            pl.BlockSpec((1, LANES), const),
            pl.BlockSpec((1, SSD_INNER), const),
            pl.BlockSpec((LANES, SSD_INNER), const),
            pl.BlockSpec((rows, SSD_INNER), cur(2)),
            pl.BlockSpec((rows, SSD_INNER), cur(0)),
            pl.BlockSpec((1, SSD_INNER), const),
            pl.BlockSpec((rows, w), cur(3)),
            pl.BlockSpec((rows, w), cur(4)), pl.BlockSpec((HALO, w), prev(4)), pl.BlockSpec((HALO, w), nxt(4)),
            pl.BlockSpec((rows, w), cur(5)), pl.BlockSpec((HALO, w), prev(5)), pl.BlockSpec((HALO, w), nxt(5)),
            pl.BlockSpec((rows, w), cur(6)),
            pl.BlockSpec((rows, D_MODEL), out_blk),
            pl.BlockSpec((3, w), const),
            pl.BlockSpec((SSD_INNER, D_MODEL), const),
            pl.BlockSpec((w, D_MODEL), const),
            pl.BlockSpec((1, D_MODEL), const),
            pl.BlockSpec((1, D_MODEL), const),
        ],
        out_specs=pl.BlockSpec((rows, D_MODEL), out_blk),
        out_shape=jax.ShapeDtypeStruct((t, D_MODEL), F32),
        scratch_shapes=[pltpu.VMEM((SSD_GROUPS, SSD_STATE, SSD_INNER // SSD_GROUPS), F32),
                        pltpu.VMEM((rows, SSD_INNER), F32),
                        pltpu.VMEM((rows, SSD_INNER), BF16),
                        pltpu.VMEM((rows, w), BF16)],
        compiler_params=pltpu.CompilerParams(
            dimension_semantics=("parallel", "arbitrary"), vmem_limit_bytes=VMEM_LIMIT),
        name="ssd_forward_out_proj",
    )(xact, dt_raw, a_row, dtb_row, dsk_row, e_mat, proj, yb, norm_g,
      proj, proj, proj, proj, proj, proj, proj, proj, x2d, sc_w, w_a, w_b, ln_g, ln_b)


_O_CQ = 0
_O_CKV = MLA_Q_RANK
_O_KRA = _O_CKV + MLA_KV_RANK
_O_KRB = _O_KRA + LANES
_O_REST = _O_KRB + LANES
_ODD_REST = MLA_WIDTH + 2 * POOL_WIDTH
_ODD_COLS = _O_REST + _ODD_REST
_HEAD_W = LANES


def _odd_in_kernel(x_ref, pos_ref, w_ref, qg_ref, kg_ref, wq_ref, wqs_ref, wkn_ref, wv_ref, frq_ref,
                   sgn_ref, vone_ref, q_ref, k_ref, v_ref, r_ref):
    proj = _dot(x_ref[...].astype(BF16), w_ref[...])
    r_ref[...] = proj[:, _O_REST:].astype(BF16)
    ang = frq_ref[...] * pos_ref[...].astype(F32)
    pad = jnp.zeros((LANES - 3 * ang.shape[0], ang.shape[1]), F32)

    def table(v, place):
        hi, lo = _split_bf16(v, 2)
        stack = jnp.concatenate([hi.astype(F32), lo.astype(F32), jnp.ones_like(v), pad], axis=0)
        return _dot(stack.T.astype(BF16), place)

    cosf = table(jnp.cos(ang), sgn_ref[0])
    sinf = table(jnp.sin(ang), sgn_ref[1])
    cqn = _rms_norm_rows(proj[:, _O_CQ:_O_CQ + MLA_Q_RANK], qg_ref[...]).astype(BF16)
    qa = _dot(cqn, wq_ref[...])
    qb = _dot(cqn, wqs_ref[...])
    ckvn = _rms_norm_rows(proj[:, _O_CKV:_O_CKV + MLA_KV_RANK], kg_ref[...]).astype(BF16)
    kn = _dot(ckvn, wkn_ref[...])
    vv = _dot(ckvn, wv_ref[...]) + vone_ref[...]
    kr = proj[:, _O_KRA:_O_KRA + LANES] * cosf + proj[:, _O_KRB:_O_KRB + LANES] * sinf
    for h in range(MLA_HEADS):
        sl = slice(h * _HEAD_W, (h + 1) * _HEAD_W)
        q_ref[h] = ((qa[:, sl] * cosf + qb[:, sl] * sinf) * (ATTN_SCALE * LOG2E)).astype(BF16)
        k_ref[h] = (kn[:, sl] + kr).astype(BF16)
        v_ref[h] = vv[:, sl].astype(BF16)


def _odd_in(x2d, pos2d, w2, qg, kg, wq, wqs, wkn, wv, frq, sgn, vone, s, tm):
    t = x2d.shape[0]
    b = t // s
    tps = s // tm
    const = lambda i: (0, 0)
    hw = MLA_HEADS * _HEAD_W
    qkv_spec = pl.BlockSpec((None, MLA_HEADS, tm, _HEAD_W), lambda i: (i // tps, 0, i % tps, 0))
    qkv_shape = jax.ShapeDtypeStruct((b, MLA_HEADS, s, _HEAD_W), BF16)
    return pl.pallas_call(
        _odd_in_kernel,
        grid=(t // tm,),
        in_specs=[
            pl.BlockSpec((tm, D_MODEL), lambda i: (i, 0)),
            pl.BlockSpec((None, 1, tm), lambda i: (i, 0, 0)),
            pl.BlockSpec((D_MODEL, _ODD_COLS), const),
            pl.BlockSpec((1, MLA_Q_RANK), const),
            pl.BlockSpec((1, MLA_KV_RANK), const),
            pl.BlockSpec((MLA_Q_RANK, hw), const),
            pl.BlockSpec((MLA_Q_RANK, hw), const),
            pl.BlockSpec((MLA_KV_RANK, hw), const),
            pl.BlockSpec((MLA_KV_RANK, hw), const),
            pl.BlockSpec(frq.shape, const),
            pl.BlockSpec(sgn.shape, lambda i: (0, 0, 0)),
            pl.BlockSpec((1, hw), const),
        ],
        out_specs=[qkv_spec, qkv_spec, qkv_spec, pl.BlockSpec((tm, _ODD_REST), lambda i: (i, 0))],
        out_shape=[qkv_shape, qkv_shape, qkv_shape, jax.ShapeDtypeStruct((t, _ODD_REST), BF16)],
        compiler_params=pltpu.CompilerParams(
            dimension_semantics=("parallel",), vmem_limit_bytes=VMEM_LIMIT),
        name="odd_in_proj_qkv",
    )(x2d, pos2d, w2, qg, kg, wq, wqs, wkn, wv, frq, sgn, vone)


def _attn_kernel(q_ref, k_ref, v_ref, o_ref, s_ref, p_ref, m_ref, a_ref, acc_ref, *, tq, tk, strip, unroll):
    seq = k_ref.shape[0]
    nq = seq // tq
    nk = seq // tk
    ipq = nk // unroll
    ncol = tk // LANES
    odd = (pl.program_id(1) % 2) == 1

    def scores(qi, kt, slot):
        qo = pl.multiple_of(qi * tq, tq)
        ko = pl.multiple_of(kt * tk, tk)
        s_ref[slot] = lax.dot_general(q_ref[pl.ds(qo, tq), :], k_ref[pl.ds(ko, tk), :],
                                      (((1,), (1,)), ((), ())), preferred_element_type=F32)

    def accumulate(kt, slot):
        vo = pl.multiple_of(kt * tk, tk)
        acc_ref[...] = acc_ref[...] * a_ref[...] + _dot(p_ref[slot], v_ref[pl.ds(vo, tk), :])

    def emit(qi):
        acc = acc_ref[...]
        denom = jnp.where(odd, acc[:, 0:1], acc[:, MLA_V:MLA_V + 1])
        o_ref[pl.ds(pl.multiple_of(qi * tq, tq), tq), :] = (acc / denom).astype(BF16)

    def softmax(slot, restart):
        for r in range(tq // strip):
            rows = slice(r * strip, (r + 1) * strip)
            blks = [s_ref[slot, rows, c * LANES:(c + 1) * LANES] for c in range(ncol)]
            mx = blks[0]
            for c in range(1, ncol):
                mx = jnp.maximum(mx, blks[c])
            m_old = m_ref[rows, :]
            if restart is not None:
                m_old = jnp.where(restart, -jnp.inf, m_old)
            m_new = jnp.maximum(m_old, jnp.max(mx, axis=1, keepdims=True))
            m_ref[rows, :] = m_new
            a_ref[rows, :] = jnp.exp2(m_old - m_new)
            for c in range(ncol):
                p_ref[slot, rows, c * LANES:(c + 1) * LANES] = jnp.exp2(blks[c] - m_new).astype(BF16)

    m_ref[...] = jnp.zeros(m_ref.shape, F32)
    a_ref[...] = jnp.ones(a_ref.shape, F32)
    acc_ref[...] = jnp.ones(acc_ref.shape, F32)
    p_ref[1] = jnp.zeros(p_ref.shape[1:], BF16)
    scores(0, 0, 0)

    def body(j, carry):
        qi = j // ipq
        jj = j % ipq
        first = jj == 0
        for u in range(unroll):
            slot = u % 2
            t = jj * unroll + u
            if u == 0:
                accumulate(jnp.where(first, nk - 1, t - 1), 1 - slot)
                emit(jnp.where(first, jnp.maximum(qi - 1, 0), qi))
            else:
                accumulate(t - 1, 1 - slot)
            if u == unroll - 1:
                last = jj == ipq - 1
                scores(jnp.where(last, jnp.minimum(qi + 1, nq - 1), qi), jnp.where(last, 0, t + 1), 1 - slot)
            else:
                scores(qi, t + 1, 1 - slot)
            softmax(slot, first if u == 0 else None)
        return carry

    lax.fori_loop(0, nq * ipq, body, 0)
    accumulate(nk - 1, 1 - unroll % 2)
    emit(nq - 1)


def _attention(q, k, v, tq, tk, strip, unroll):
    b, h, s, w = q.shape
    assert unroll % 2 == 0 and (s // tk) % unroll == 0 and s % tq == 0 and tq % strip == 0
    spec = pl.BlockSpec((None, None, s, w), lambda bi, hi: (bi, hi, 0, 0))
    return pl.pallas_call(
        functools.partial(_attn_kernel, tq=tq, tk=tk, strip=strip, unroll=unroll),
        grid=(b, h),
        in_specs=[spec, spec, spec],
        out_specs=spec,
        out_shape=jax.ShapeDtypeStruct((b, h, s, w), BF16),
        scratch_shapes=[
            pltpu.VMEM((2, tq, tk), F32),
            pltpu.VMEM((2, tq, tk), BF16),
            pltpu.VMEM((tq, LANES), F32),
            pltpu.VMEM((tq, LANES), F32),
            pltpu.VMEM((tq, w), F32),
        ],
        compiler_params=pltpu.CompilerParams(
            dimension_semantics=("parallel", "parallel"), vmem_limit_bytes=VMEM_LIMIT),
        name="mla_attention",
    )(q, k, v)


def _odd_out_kernel(o_ref, gc_ref, um_ref, up_ref, un_ref, gd_ref, x_ref, pw_ref, ps_ref, wc_ref, wd_ref,
                    lg_ref, lb_ref, out_ref, *, tiles_per_seq, seq):
    tm = x_ref.shape[0]
    i = pl.program_id(0) % tiles_per_seq
    lane = lax.broadcasted_iota(jnp.int32, (tm, LANES), 1)
    low = lane < MLA_V
    gc = _silu(gc_ref[...].astype(F32))
    ycs = []
    for j in range(MLA_HEADS // 2):
        pair = jnp.where(low, o_ref[2 * j], o_ref[2 * j + 1]).astype(F32)
        ycs.append((pair * gc[:, j * LANES:(j + 1) * LANES]).astype(BF16))
    acc = _dot(jnp.concatenate(ycs, axis=1), wc_ref[...])
    yds = []

    pos = i * tm + lax.broadcasted_iota(jnp.int32, (tm, LANES), 0)
    gd = _silu(gd_ref[...].astype(F32))
    for gi, w in enumerate(POOL_WINDOWS):
        cs = slice(gi * POOL_GROUP, (gi + 1) * POOL_GROUP)
        um = um_ref[:, cs].astype(F32)
        ext = jnp.concatenate([jnp.where(i > 0, up_ref[:, cs].astype(F32), 0.0), um,
                               jnp.where(i < tiles_per_seq - 1, un_ref[:, cs].astype(F32), 0.0)], axis=0)
        run, span = ext, 1
        while span < w:
            run = run + pltpu.roll(run, span, axis=0)
            span *= 2
        lead = w - w // 2 - 1
        wsum = run[HALO + lead:HALO + lead + tm]
        cnt = (jnp.minimum(pos + (w - w // 2), seq) - jnp.maximum(pos - w // 2, 0)).astype(F32)
        pooled = wsum / cnt - um
        yds.append((_dot(pooled.astype(BF16), pw_ref[gi]) * ps_ref[:, cs] * gd[:, cs]).astype(BF16))
    acc = acc + _dot(jnp.concatenate(yds, axis=1), wd_ref[...])
    out_ref[...] = _layer_norm_rows(ALPHA * x_ref[...] + acc, lg_ref[...], lb_ref[...])


def _odd_out(o, rest, x2d, pool_w, pool_scale, w_c, w_d, ln_g, ln_b, s, tm):
    t = x2d.shape[0]
    tps = s // tm
    hb = tm // HALO
    nh = t // HALO
    const = lambda i: (0, 0)
    pw = POOL_WIDTH
    return pl.pallas_call(
        functools.partial(_odd_out_kernel, tiles_per_seq=tps, seq=s),
        grid=(t // tm,),
        in_specs=[
            pl.BlockSpec((None, MLA_HEADS, tm, _HEAD_W), lambda i: (i // tps, 0, i % tps, 0)),
            pl.BlockSpec((tm, pw), lambda i: (i, 0)),
            pl.BlockSpec((tm, pw), lambda i: (i, 1)),
            pl.BlockSpec((HALO, pw), lambda i: (jnp.maximum(i * hb - 1, 0), 1)),
            pl.BlockSpec((HALO, pw), lambda i: (jnp.minimum((i + 1) * hb, nh - 1), 1)),
            pl.BlockSpec((tm, pw), lambda i: (i, 2)),
            pl.BlockSpec((tm, D_MODEL), lambda i: (i, 0)),
            pl.BlockSpec((len(POOL_WINDOWS), POOL_GROUP, POOL_GROUP), lambda i: (0, 0, 0)),
            pl.BlockSpec((1, pw), const),
            pl.BlockSpec((MLA_WIDTH, D_MODEL), const),
            pl.BlockSpec((pw, D_MODEL), const),
            pl.BlockSpec((1, D_MODEL), const),
            pl.BlockSpec((1, D_MODEL), const),
        ],
        out_specs=pl.BlockSpec((tm, D_MODEL), lambda i: (i, 0)),
        out_shape=jax.ShapeDtypeStruct((t, D_MODEL), F32),
        compiler_params=pltpu.CompilerParams(
            dimension_semantics=("parallel",), vmem_limit_bytes=VMEM_LIMIT),
        name="odd_out_proj",
    )(o, rest, rest, rest, rest, rest, x2d, pool_w, pool_scale, w_c, w_d, ln_g, ln_b)


def _rep3(v):
    z = jnp.zeros((LANES - 3 * SSD_HEADS,), F32)
    return jnp.concatenate([v, v, v, z])[None, :]


def _even_params(w_in, conv_w, conv_b, a_log, dt_bias, d_skip, norm_g, sc_conv_w, w_out):
    o_z = 0
    o_xbc = SSD_INNER
    o_dt = o_xbc + SSD_XBC
    o_sc = o_dt + 2 * SSD_HEADS
    w_main = jnp.concatenate([w_in[:, o_xbc:o_dt], w_in[:, o_z:o_xbc], w_in[:, o_sc:]], axis=1).astype(BF16)
    zpad = jnp.zeros((D_MODEL, LANES - 3 * SSD_HEADS), F32)
    wdt = []
    for k in range(2):
        wk = w_in[:, o_dt + k * SSD_HEADS:o_dt + (k + 1) * SSD_HEADS]
        wdt.append(jnp.concatenate([wk, wk, wk, zpad], axis=1))
    wdt = jnp.concatenate(wdt, axis=1)
    wdt_hi = wdt.astype(BF16)
    wdt_lo = (wdt - wdt_hi.astype(F32)).astype(BF16)
    a_rows = [_rep3(-jnp.exp(a_log[k].astype(F32)) * LOG2E) for k in range(2)]
    dtb_rows = [_rep3(dt_bias[k].astype(F32)) for k in range(2)]
    dsk_rows = [jnp.repeat(d_skip[k].astype(F32), SSD_HEAD_DIM)[None, :] for k in range(2)]
    src = jnp.arange(LANES)[:, None]
    dst = jnp.arange(SSD_INNER)[None, :] // SSD_HEAD_DIM
    e_mat = jnp.where((src < 2 * SSD_HEADS) & (src % SSD_HEADS == dst), 1.0, 0.0).astype(BF16)
    trow = jnp.arange(SSD_CHUNK)[:, None]
    wcol = jnp.arange(SSD_CHUNK + 2 * HALO_B)[None, :]
    shifts = jnp.stack([jnp.where(wcol == trow + HALO_B + (k - SSD_CONV_LEFT), 1.0, 0.0)
                        for k in range(SSD_CONV) if k != SSD_CONV_LEFT]).astype(BF16)
    return dict(shifts=shifts,
        w_main=w_main, wdt_hi=wdt_hi, wdt_lo=wdt_lo, conv_w=conv_w.astype(F32), conv_b=conv_b[None, :].astype(F32),
        a_rows=a_rows, dtb_rows=dtb_rows, dsk_rows=dsk_rows, e_mat=e_mat, norm_g=norm_g[None, :].astype(F32),
        sc_w=sc_conv_w.astype(F32), w_a=w_out[:SSD_INNER].astype(BF16), w_b=w_out[SSD_INNER:].astype(BF16))


def _odd_params(w_in, q_norm_g, w_uq, kv_norm_g, w_ukv, pool_w, pool_scale, w_out):
    o_ckv = MLA_Q_RANK
    o_kr = o_ckv + MLA_KV_RANK
    o_rest = o_kr + MLA_ROPE
    half = MLA_ROPE // 2
    z64 = jnp.zeros((D_MODEL, MLA_NOPE), F32)
    z32 = jnp.zeros((D_MODEL, LANES - MLA_NOPE - MLA_ROPE), F32)
    kr1 = w_in[:, o_kr:o_kr + half]
    kr2 = w_in[:, o_kr + half:o_kr + MLA_ROPE]
    w2 = jnp.concatenate([w_in[:, :o_kr], z64, kr1, kr2, z32, z64, kr2, kr1, z32, w_in[:, o_rest:]],
                         axis=1).astype(BF16)
    qd = MLA_NOPE + MLA_ROPE
    w3 = w_uq.reshape(MLA_Q_RANK, MLA_HEADS, qd)
    zq = lambda n: jnp.zeros((MLA_Q_RANK, MLA_HEADS, n), F32)
    wq = jnp.concatenate([w3, zq(_HEAD_W - qd)], axis=-1).reshape(MLA_Q_RANK, -1).astype(BF16)
    wqs = jnp.concatenate([zq(MLA_NOPE), w3[..., MLA_NOPE + half:], w3[..., MLA_NOPE:MLA_NOPE + half],
                           zq(_HEAD_W - qd)], axis=-1).reshape(MLA_Q_RANK, -1).astype(BF16)
    w4 = w_ukv.reshape(MLA_KV_RANK, MLA_HEADS, MLA_NOPE + MLA_V)
    zk = jnp.zeros((MLA_KV_RANK, MLA_HEADS, _HEAD_W - MLA_NOPE), F32)
    wkn = jnp.concatenate([w4[..., :MLA_NOPE], zk], axis=-1).reshape(MLA_KV_RANK, -1).astype(BF16)
    zv = jnp.zeros((MLA_KV_RANK, _HEAD_W - MLA_V), F32)
    wv_blocks = []
    vone = []
    for h in range(MLA_HEADS):
        vh = w4[:, h, MLA_NOPE:]
        wv_blocks.append(jnp.concatenate([vh, zv] if h % 2 == 0 else [zv, vh], axis=-1))
        vone.append(jnp.zeros((_HEAD_W,), F32).at[MLA_V if h % 2 == 0 else 0].set(1.0))
    wv = jnp.concatenate(wv_blocks, axis=-1).astype(BF16)
    vone = jnp.concatenate(vone)[None, :]
    inv_freq = ROPE_THETA ** (-jnp.arange(half, dtype=F32) / half)
    frq = inv_freq[:, None]
    src = jnp.arange(LANES)[:, None]
    lane = jnp.arange(LANES)[None, :]
    rot = (lane >= MLA_NOPE) & (lane < qd) & (src < 2 * half) & ((lane - MLA_NOPE) % half == src % half)
    place_cos = jnp.where(rot | ((lane < MLA_NOPE) & (src == 2 * half)), 1.0, 0.0)
    place_sin = jnp.where(rot, jnp.where(lane < MLA_NOPE + half, -1.0, 1.0), 0.0)
    sgn = jnp.stack([place_cos, place_sin]).astype(BF16)
    return dict(
        w2=w2, qg=q_norm_g[None, :].astype(F32), kg=kv_norm_g[None, :].astype(F32), wq=wq, wqs=wqs, wkn=wkn,
        wv=wv, vone=vone, frq=frq, sgn=sgn, pool_w=pool_w.astype(BF16), pool_scale=pool_scale[None, :].astype(F32),
        w_c=w_out[:MLA_WIDTH].astype(BF16), w_d=w_out[MLA_WIDTH:].astype(BF16))


def _pick(n, pref):
    t = min(pref, n)
    while n % t:
        t //= 2
    return t


def _even_layer(x2d, s, p, ln_g, ln_b):
    t = x2d.shape[0]
    proj, dt_raw = _even_in_proj(x2d, p["w_main"], p["wdt_hi"], p["wdt_lo"], _pick(t, 1024), 3584)
    yb, xact = _ssd_backward_sweep(proj, dt_raw, p["shifts"], p["conv_w"], p["conv_b"], p["a_rows"][1],
                                   p["dtb_rows"][1], p["dsk_rows"][1], p["e_mat"], s)
    return _ssd_forward_out(xact, dt_raw, proj, yb, x2d, p["a_rows"][0], p["dtb_rows"][0], p["dsk_rows"][0],
                            p["e_mat"], p["norm_g"], p["sc_w"], p["w_a"], p["w_b"], ln_g[None, :], ln_b[None, :], s)


def _odd_layer(x2d, pos2d, s, p, ln_g, ln_b):
    tm = _pick(s, 512)
    q, k, v, rest = _odd_in(x2d, pos2d.reshape(-1, 1, tm), p["w2"], p["qg"], p["kg"], p["wq"], p["wqs"], p["wkn"], p["wv"],
                            p["frq"], p["sgn"], p["vone"], s, tm)
    o = _attention(q, k, v, _pick(s, 512), _pick(s // 32, 512), 64, 32)
    return _odd_out(o, rest, x2d, p["pool_w"], p["pool_scale"], p["w_c"], p["w_d"], ln_g[None, :], ln_b[None, :],
                    s, tm)


def kernel(x, positions, ev_w_in, ev_conv_w, ev_conv_b, ev_a_log, ev_dt_bias, ev_d_skip, ev_norm_g, ev_sc_conv_w,
           ev_w_out, ev_ln_g, ev_ln_b, od_w_in, od_q_norm_g, od_w_uq, od_kv_norm_g, od_w_ukv, od_pool_w,
           od_pool_scale, od_w_out, od_ln_g, od_ln_b):
    b, s, d = x.shape
    x2d = x.reshape(b * s, d)
    pos2d = positions.reshape(b * s, 1)
    for layer in range(DEPTH):
        i = layer // 2
        if layer % 2 == 0:
            p = _even_params(ev_w_in[i], ev_conv_w[i], ev_conv_b[i], ev_a_log[i], ev_dt_bias[i], ev_d_skip[i],
                             ev_norm_g[i], ev_sc_conv_w[i], ev_w_out[i])
            x2d = _even_layer(x2d, s, p, ev_ln_g[i], ev_ln_b[i])
        else:
            p = _odd_params(od_w_in[i], od_q_norm_g[i], od_w_uq[i], od_kv_norm_g[i], od_w_ukv[i], od_pool_w[i],
                            od_pool_scale[i], od_w_out[i])
            x2d = _odd_layer(x2d, pos2d, s, p, od_ln_g[i], od_ln_b[i])
    return x2d.reshape(b, s, d)
```
